```python
import math
import jax, jax.numpy as jnp
from jax import lax
import numpy as np

D_MODEL = 1024
BATCH = 8
SEQ = 8192
DEPTH = 4

GRID_W = 64
CTX_LEN = 256
HEAD_DIM = 64
Q_BLOCK = 128
ROPE_THETA = 10000.0
EPS = 1e-6
N_MOD = 6

A_HEADS = D_MODEL // 256
A_VDIM = 2 * HEAD_DIM
B_HEADS = D_MODEL // 128
B_KV_HEADS = B_HEADS // 4
B_GROUP = B_HEADS // B_KV_HEADS
A_Q = A_HEADS * 2 * HEAD_DIM
A_K = A_HEADS * 2 * HEAD_DIM
A_V = A_HEADS * A_VDIM
B_Q = B_HEADS * HEAD_DIM
B_K = B_KV_HEADS * HEAD_DIM
B_V = B_KV_HEADS * HEAD_DIM
EVEN_SPLITS = [A_Q, A_Q + A_K, A_Q + A_K + A_V, A_Q + A_K + A_V + B_Q, A_Q + A_K + A_V + B_Q + B_K]
EVEN_IN = A_Q + A_K + A_V + B_Q + B_K + B_V
A_OUT = A_HEADS * A_VDIM
B_OUT = B_HEADS * HEAD_DIM

MLA_HEADS = D_MODEL // 64
MLA_NOPE = 64
MLA_ROPE = 32
MLA_VDIM = 64
MLA_Q_RANK = D_MODEL // 4
MLA_KV_RANK = D_MODEL // 8
ODD_IN = MLA_Q_RANK + MLA_KV_RANK + MLA_ROPE
MLA_OUT = MLA_HEADS * MLA_VDIM

N_EXPERTS = 16
EC_CAPACITY_FACTOR = 2
D_EXPERT = D_MODEL

N_EVEN = (DEPTH + 1) // 2
N_ODD = DEPTH // 2

kernel_name = 'hybrid_diffattn_gqa_mla_ec_moe_prefix_trunk'


def rmsnorm(x, g):
    xf = x.astype(jnp.float32)
    y = xf * lax.rsqrt(jnp.mean(xf * xf, axis=-1, keepdims=True) + EPS)
    return y.astype(x.dtype) * g


def modulate(h, shift, scale):
    return h * (1 + scale) + shift


def grid_angles(n_tokens, rot_dim):
    rows = n_tokens // GRID_W
    row = jnp.repeat(jnp.arange(rows, dtype=jnp.float32), GRID_W)
    col = jnp.tile(jnp.arange(GRID_W, dtype=jnp.float32), rows)
    half = rot_dim // 2
    freqs = ROPE_THETA ** (-jnp.arange(0, half, 2, dtype=jnp.float32) / half)
    return row[:, None] * freqs, col[:, None] * freqs


def rope_1d(x, ang):
    ang = ang.reshape((ang.shape[0],) + (1,) * (x.ndim - 3) + (ang.shape[1],))
    cos = jnp.cos(ang).astype(x.dtype)
    sin = jnp.sin(ang).astype(x.dtype)
    x1, x2 = jnp.split(x, 2, axis=-1)
    return jnp.concatenate([x1 * cos - x2 * sin, x2 * cos + x1 * sin], axis=-1)


def rope_2d(x, ang_row, ang_col):
    xr, xc = jnp.split(x, 2, axis=-1)
    return jnp.concatenate([rope_1d(xr, ang_row), rope_1d(xc, ang_col)], axis=-1)


def sweep_query_blocks(fn, *qs):
    bsz, t = qs[0].shape[:2]
    nb = t // Q_BLOCK
    blocks = tuple(jnp.moveaxis(q.reshape((bsz, nb, Q_BLOCK) + q.shape[2:]), 1, 0) for q in qs)
    out = lax.map(lambda qb: fn(*qb), blocks)
    return jnp.moveaxis(out, 0, 1).reshape((bsz, t) + out.shape[3:])


def diff_attend(q, k, v, lam, scale):
    s = jnp.einsum('bqhcd,bkhcd->bhcqk', q, k).astype(jnp.float32) * scale
    p = jax.nn.softmax(s, axis=-1)
    w = (p[:, :, 0] - lam * p[:, :, 1]).astype(v.dtype)
    return jnp.einsum('bhqk,bkhd->bqhd', w, v)


def gqa_attend(q, k, v, scale):
    s = jnp.einsum('bqhgd,bkhd->bhgqk', q, k).astype(jnp.float32) * scale
    p = jax.nn.softmax(s, axis=-1).astype(v.dtype)
    return jnp.einsum('bhgqk,bkhd->bqhgd', p, v)


def mla_attend(qn, qr, kn, kr, v, scale):
    s = jnp.einsum('bqhd,bkhd->bhqk', qn, kn) + jnp.einsum('bqhr,bkr->bhqk', qr, kr)
    p = jax.nn.softmax(s.astype(jnp.float32) * scale, axis=-1).astype(v.dtype)
    return jnp.einsum('bhqk,bkhd->bqhd', p, v)


def even_project(h, w_in, a_qk, b_qk):
    bsz, t = h.shape[:2]
    qa, ka, va, qb, kb, vb = jnp.split(h @ w_in, EVEN_SPLITS, axis=-1)
    qa = rmsnorm(qa.reshape(bsz, t, A_HEADS, 2, HEAD_DIM), a_qk[0])
    ka = rmsnorm(ka.reshape(bsz, t, A_HEADS, 2, HEAD_DIM), a_qk[1])
    va = va.reshape(bsz, t, A_HEADS, A_VDIM)
    qb = rmsnorm(qb.reshape(bsz, t, B_KV_HEADS, B_GROUP, HEAD_DIM), b_qk[0])
    kb = rmsnorm(kb.reshape(bsz, t, B_KV_HEADS, HEAD_DIM), b_qk[1])
    vb = vb.reshape(bsz, t, B_KV_HEADS, HEAD_DIM)
    return qa, ka, va, qb, kb, vb


def even_mixer(h_lat, h_ctx, w_in, a_qk, lam_vecs, a_subln, b_qk, w_out, layer_idx, ctx_out):
    lam_init = 0.8 - 0.6 * math.exp(-0.3 * layer_idx)
    lv = lam_vecs.astype(jnp.float32)
    lam = jnp.exp(jnp.sum(lv[0] * lv[1])) - jnp.exp(jnp.sum(lv[2] * lv[3])) + lam_init
    scale = HEAD_DIM ** -0.5
    qa_l, ka_l, va_l, qb_l, kb_l, vb_l = even_project(h_lat, w_in, a_qk, b_qk)
    qa_c, ka_c, va_c, qb_c, kb_c, vb_c = even_project(h_ctx, w_in, a_qk, b_qk)
    ang_row, ang_col = grid_angles(h_lat.shape[1], HEAD_DIM)
    qa_l, ka_l, qb_l, kb_l = (rope_2d(t, ang_row, ang_col) for t in (qa_l, ka_l, qb_l, kb_l))
    ka = jnp.concatenate([ka_l, ka_c], axis=1)
    va = jnp.concatenate([va_l, va_c], axis=1)
    kb = jnp.concatenate([kb_l, kb_c], axis=1)
    vb = jnp.concatenate([vb_l, vb_c], axis=1)
    oa_l = sweep_query_blocks(lambda q: diff_attend(q, ka, va, lam, scale), qa_l)
    ob_l = sweep_query_blocks(lambda q: gqa_attend(q, kb, vb, scale), qb_l)

    def merge(oa, ob):
        bsz, t = oa.shape[:2]
        oa = rmsnorm(oa, a_subln) * (1.0 - lam_init)
        return jnp.concatenate([oa.reshape(bsz, t, A_OUT), ob.reshape(bsz, t, B_OUT)], axis=-1) @ w_out

    y_lat = merge(oa_l, ob_l)
    y_ctx = None
    if ctx_out:
        y_ctx = merge(diff_attend(qa_c, ka_c, va_c, lam, scale), gqa_attend(qb_c, kb_c, vb_c, scale))
    return y_lat, y_ctx


def mla_project(h, w_in, q_norm, w_q_up, kv_norm, w_kv_up, qk_norm):
    bsz, t = h.shape[:2]
    cq, ckv, kr = jnp.split(h @ w_in, [MLA_Q_RANK, MLA_Q_RANK + MLA_KV_RANK], axis=-1)
    q = (rmsnorm(cq, q_norm) @ w_q_up).reshape(bsz, t, MLA_HEADS, MLA_NOPE + MLA_ROPE)
    kv = (rmsnorm(ckv, kv_norm) @ w_kv_up).reshape(bsz, t, MLA_HEADS, MLA_NOPE + MLA_VDIM)
    qn = rmsnorm(q[..., :MLA_NOPE], qk_norm[0, :MLA_NOPE])
    qr = rmsnorm(q[..., MLA_NOPE:], qk_norm[0, MLA_NOPE:])
    kn = rmsnorm(kv[..., :MLA_NOPE], qk_norm[1, :MLA_NOPE])
    v = kv[..., MLA_NOPE:]
    kr = rmsnorm(kr, qk_norm[1, MLA_NOPE:])
    return qn, qr, kn, kr, v


def odd_mixer(h_lat, h_ctx, w_in, q_norm, w_q_up, kv_norm, w_kv_up, qk_norm, w_out, ctx_out):
    scale = (MLA_NOPE + MLA_ROPE) ** -0.5
    qn_l, qr_l, kn_l, kr_l, v_l = mla_project(h_lat, w_in, q_norm, w_q_up, kv_norm, w_kv_up, qk_norm)
    qn_c, qr_c, kn_c, kr_c, v_c = mla_project(h_ctx, w_in, q_norm, w_q_up, kv_norm, w_kv_up, qk_norm)
    ang_row, ang_col = grid_angles(h_lat.shape[1], MLA_ROPE)
    qr_l = rope_2d(qr_l, ang_row, ang_col)
    kr_l = rope_2d(kr_l, ang_row, ang_col)
    kn = jnp.concatenate([kn_l, kn_c], axis=1)
    kr = jnp.concatenate([kr_l, kr_c], axis=1)
    v = jnp.concatenate([v_l, v_c], axis=1)
    o_l = sweep_query_blocks(lambda a, b: mla_attend(a, b, kn, kr, v, scale), qn_l, qr_l)
    bsz, t = h_lat.shape[:2]
    y_lat = o_l.reshape(bsz, t, MLA_OUT) @ w_out
    y_ctx = None
    if ctx_out:
        o_c = mla_attend(qn_c, qr_c, kn_c, kr_c, v_c, scale)
        y_ctx = o_c.reshape(bsz, h_ctx.shape[1], MLA_OUT) @ w_out
    return y_lat, y_ctx


def ec_moe(h, w_router, w_gate, w_up, w_down):
    bsz, t, d = h.shape
    cap = EC_CAPACITY_FACTOR * t // N_EXPERTS
    aff = jax.nn.softmax(jnp.einsum('btd,de->bte', h, w_router).astype(jnp.float32), axis=-1)
    gates, idx = lax.top_k(jnp.swapaxes(aff, 1, 2), cap)
    xs = jax.vmap(lambda hb, ib: hb[ib])(h, idx)
    a = jnp.einsum('becd,edf->becf', xs, w_gate)
    u = jnp.einsum('becd,edf->becf', xs, w_up)
    y = jnp.einsum('becf,efd->becd', jax.nn.silu(a) * u, w_down)
    y = y * gates[..., None].astype(y.dtype)
    return jax.vmap(lambda yb, ib: jnp.zeros((t, d), y.dtype).at[ib.reshape(-1)].add(yb.reshape(-1, d)))(y, idx)


def setup_inputs(seed: int = 0) -> dict:
    key = jax.random.key(seed)
    ks = iter(jax.random.split(key, 32))

    def nrm(shape, scale):
        return jax.random.normal(next(ks), shape, jnp.float32) * scale

    def gain(shape):
        return 1.0 + nrm(shape, 0.1)

    D = D_MODEL
    return {
        'x': nrm((BATCH, SEQ, D), 1.0),
        'c': nrm((BATCH, D), 1.0),
        'ctx': nrm((BATCH, CTX_LEN, D), 1.0),
        'c_ctx': nrm((D,), 1.0),
        'w_ada': nrm((DEPTH, D, N_MOD * D), 0.5 * D ** -0.5),
        'b_ada': nrm((DEPTH, N_MOD * D), 0.01),
        'norm_mix': gain((DEPTH, D)),
        'norm_ffn': gain((DEPTH, D)),
        'w_in_even': nrm((N_EVEN, D, EVEN_IN), D ** -0.5),
        'a_qk_norm': gain((N_EVEN, 2, HEAD_DIM)),
        'diff_lambda': nrm((N_EVEN, 4, HEAD_DIM), 0.1),
        'a_subln': gain((N_EVEN, A_VDIM)),
        'b_qk_norm': gain((N_EVEN, 2, HEAD_DIM)),
        'w_out_even': nrm((N_EVEN, A_OUT + B_OUT, D), (A_OUT + B_OUT) ** -0.5),
        'w_in_odd': nrm((N_ODD, D, ODD_IN), D ** -0.5),
        'mla_q_norm': gain((N_ODD, MLA_Q_RANK)),
        'w_q_up': nrm((N_ODD, MLA_Q_RANK, MLA_HEADS * (MLA_NOPE + MLA_ROPE)), MLA_Q_RANK ** -0.5),
        'mla_kv_norm': gain((N_ODD, MLA_KV_RANK)),
        'w_kv_up': nrm((N_ODD, MLA_KV_RANK, MLA_HEADS * (MLA_NOPE + MLA_VDIM)), MLA_KV_RANK ** -0.5),
        'mla_qk_norm': gain((N_ODD, 2, MLA_NOPE + MLA_ROPE)),
        'w_out_odd': nrm((N_ODD, MLA_OUT, D), MLA_OUT ** -0.5),
        'w_router': nrm((DEPTH, D, N_EXPERTS), D ** -0.5),
        'w_exp_gate': nrm((DEPTH, N_EXPERTS, D, D_EXPERT), D ** -0.5),
        'w_exp_up': nrm((DEPTH, N_EXPERTS, D, D_EXPERT), D ** -0.5),
        'w_exp_down': nrm((DEPTH, N_EXPERTS, D_EXPERT, D), D_EXPERT ** -0.5),
    }


def reference(x, c, ctx, c_ctx, w_ada, b_ada, norm_mix, norm_ffn, w_in_even, a_qk_norm, diff_lambda,
              a_subln, b_qk_norm, w_out_even, w_in_odd, mla_q_norm, w_q_up, mla_kv_norm, w_kv_up,
              mla_qk_norm, w_out_odd, w_router, w_exp_gate, w_exp_up, w_exp_down):
    bsz = x.shape[0]
    sc = jax.nn.silu(c)
    sc_ctx = jax.nn.silu(c_ctx)
    for i in range(DEPTH):
        last = i == DEPTH - 1
        m = (sc @ w_ada[i] + b_ada[i]).reshape(bsz, 1, N_MOD, D_MODEL)
        mc = (sc_ctx @ w_ada[i] + b_ada[i]).reshape(N_MOD, D_MODEL)
        h_lat = modulate(rmsnorm(x, norm_mix[i]), m[:, :, 0], m[:, :, 1])
        h_ctx = modulate(rmsnorm(ctx, norm_mix[i]), mc[0], mc[1])
        if i % 2 == 0:
            j = i // 2
            y_lat, y_ctx = even_mixer(h_lat, h_ctx, w_in_even[j], a_qk_norm[j], diff_lambda[j], a_subln[j],
                                      b_qk_norm[j], w_out_even[j], i, not last)
        else:
            j = i // 2
            y_lat, y_ctx = odd_mixer(h_lat, h_ctx, w_in_odd[j], mla_q_norm[j], w_q_up[j], mla_kv_norm[j],
                                     w_kv_up[j], mla_qk_norm[j], w_out_odd[j], not last)
        x = x + m[:, :, 2] * y_lat
        h_lat = modulate(rmsnorm(x, norm_ffn[i]), m[:, :, 3], m[:, :, 4])
        x = x + m[:, :, 5] * ec_moe(h_lat, w_router[i], w_exp_gate[i], w_exp_up[i], w_exp_down[i])
        if not last:
            ctx = ctx + mc[2] * y_ctx
            h_ctx = modulate(rmsnorm(ctx, norm_ffn[i]), mc[3], mc[4])
            ctx = ctx + mc[5] * ec_moe(h_ctx, w_router[i], w_exp_gate[i], w_exp_up[i], w_exp_down[i])
    return x
```

```python
import functools
import math

import numpy as np
import jax
import jax.numpy as jnp
from jax import lax
from jax.experimental import pallas as pl
from jax.experimental.pallas import tpu as pltpu

F32 = jnp.float32
I32 = jnp.int32
MXU_DTYPE = jnp.bfloat16

D_MODEL = 1024
DEPTH = 4
GRID_W = 64
HEAD_DIM = 64
ROPE_THETA = 10000.0
EPS = 1e-6
N_MOD = 6
N_EXPERTS = 16
EC_CAPACITY_FACTOR = 2
MLA_HEADS = 16
MLA_NOPE = 64
MLA_ROPE = 32
MLA_Q_RANK = 256
MLA_KV_RANK = 128

LANE = 128
BF16_SUBLANES = 16
MOD_ROWS = 16
VMEM_LIMIT = 52 * 1024 * 1024
HIGHEST = lax.Precision.HIGHEST


def _dot(a, b):
    return jnp.dot(a, b, preferred_element_type=F32)


def _dot_nt(a, b, precision=None):
    return lax.dot_general(a, b, (((1,), (1,)), ((), ())), preferred_element_type=F32, precision=precision)


def _params(sem, vmem=VMEM_LIMIT):
    return pltpu.CompilerParams(dimension_semantics=sem, vmem_limit_bytes=vmem)


def _norm_mod(x, g, shift, scale):
    ms = jnp.mean(x * x, axis=-1, keepdims=True)
    return (x * lax.rsqrt(ms + EPS)) * g * (1.0 + scale) + shift


def _mod_kernel(c_ref, w_ref, b_ref, o_ref):
    c = c_ref[...]
    sc = c * jax.nn.sigmoid(c)
    o_ref[0] = jnp.dot(sc, w_ref[0], preferred_element_type=F32, precision=HIGHEST) + b_ref[0]


def _adaln(c_rows, w_ada, b_ada):
    depth, d, n = w_ada.shape
    nb = n // 4
    return pl.pallas_call(
        _mod_kernel,
        out_shape=jax.ShapeDtypeStruct((depth, MOD_ROWS, n), F32),
        grid=(depth, n // nb),
        in_specs=[pl.BlockSpec((MOD_ROWS, d), lambda i, j: (0, 0)),
                  pl.BlockSpec((1, d, nb), lambda i, j: (i, 0, j)),
                  pl.BlockSpec((1, 1, nb), lambda i, j: (i, 0, j))],
        out_specs=pl.BlockSpec((1, MOD_ROWS, nb), lambda i, j: (i, 0, j)),
        compiler_params=_params(("arbitrary", "arbitrary")),
        name="adaln",
    )(c_rows, w_ada, b_ada.reshape(depth, 1, n))


def _seg_normrope(blk, seg, inv_n, gain, tabs, dist):
    ss = _dot((blk * blk).astype(MXU_DTYPE), seg)
    xn = blk * lax.rsqrt(ss * inv_n + EPS) * gain
    if tabs is not None:
        cos, sin_up, sin_dn = tabs
        xn = xn * cos + pltpu.roll(xn, dist, 1) * sin_up + pltpu.roll(xn, LANE - dist, 1) * sin_dn
    return xn


def _proj_even_kernel(x_ref, mod_ref, g_ref, w_ref, seg_ref, gains_ref, cos_ref, sup_ref, sdn_ref,
                      q_ref, k_ref, v_ref):
    mod = mod_ref[0]
    h = _norm_mod(x_ref[0], g_ref[...], mod[0:1], mod[1:2]).astype(MXU_DTYPE)
    y = _dot(h, w_ref[...])
    seg = seg_ref[...]
    tabs = (cos_ref[...], sup_ref[...], sdn_ref[...])
    inv_n = 1.0 / HEAD_DIM
    scale = HEAD_DIM ** -0.5
    lane = lax.broadcasted_iota(I32, (1, LANE), 1)
    first = (lane < HEAD_DIM).astype(F32)
    second = 1.0 - first
    dist = HEAD_DIM // 4

    def blk(j):
        return y[:, LANE * j:LANE * (j + 1)]

    for hh in range(4):
        qa = _seg_normrope(blk(hh), seg, inv_n, gains_ref[0:1], tabs, dist) * scale
        q_ref[0, 2 * hh] = (qa * first).astype(q_ref.dtype)
        q_ref[0, 2 * hh + 1] = (qa * second).astype(q_ref.dtype)
        ka = _seg_normrope(blk(4 + hh), seg, inv_n, gains_ref[1:2], tabs, dist)
        k_ref[0, hh] = ka.astype(k_ref.dtype)
        v_ref[0, hh] = blk(8 + hh).astype(v_ref.dtype)
    for hh in range(8):
        qb = _seg_normrope(blk(12 + hh), seg, inv_n, gains_ref[2:3], tabs, dist) * scale
        q_ref[0, 8 + hh] = qb.astype(q_ref.dtype)
    kb = _seg_normrope(blk(20), seg, inv_n, gains_ref[3:4], tabs, dist)
    k_ref[0, 4] = kb.astype(k_ref.dtype)
    v_ref[0, 4] = blk(21).astype(v_ref.dtype)


def _proj_odd_kernel(x_ref, mod_ref, g_ref, w1_ref, qn_ref, wq_ref, kvn_ref, wk_ref, wv_ref, seg_ref,
                     invn_ref, gains_ref, cos_ref, sup_ref, sdn_ref, q_ref, k_ref, v_ref):
    mod = mod_ref[0]
    h = _norm_mod(x_ref[0], g_ref[...], mod[0:1], mod[1:2]).astype(MXU_DTYPE)
    y1 = _dot(h, w1_ref[...])
    cq = y1[:, :MLA_Q_RANK]
    cqn = (cq * lax.rsqrt(jnp.mean(cq * cq, axis=-1, keepdims=True) + EPS) * qn_ref[...]).astype(MXU_DTYPE)
    ckv = y1[:, MLA_Q_RANK:MLA_Q_RANK + MLA_KV_RANK]
    ckvn = (ckv * lax.rsqrt(jnp.mean(ckv * ckv, axis=-1, keepdims=True) + EPS) * kvn_ref[...]).astype(MXU_DTYPE)
    qf = _dot(cqn, wq_ref[...])
    kf = _dot(ckvn, wk_ref[...])
    vf = _dot(ckvn, wv_ref[...])
    seg = seg_ref[...]
    inv_n = invn_ref[...]
    tabs = (cos_ref[...], sup_ref[...], sdn_ref[...])
    scale = (MLA_NOPE + MLA_ROPE) ** -0.5
    dist = MLA_ROPE // 4
    kr = _seg_normrope(y1[:, MLA_Q_RANK + MLA_KV_RANK:], seg, inv_n, gains_ref[2:3], tabs, dist)
    for hh in range(MLA_HEADS):
        qh = _seg_normrope(qf[:, LANE * hh:LANE * (hh + 1)], seg, inv_n, gains_ref[0:1], tabs, dist) * scale
        q_ref[0, hh] = qh.astype(q_ref.dtype)
        kh = _seg_normrope(kf[:, LANE * hh:LANE * (hh + 1)], seg, inv_n, gains_ref[1:2], None, 0) + kr
        k_ref[0, hh] = kh.astype(k_ref.dtype)
    for j in range(MLA_HEADS // 2):
        v_ref[0, j] = vf[:, LANE * j:LANE * (j + 1)].astype(v_ref.dtype)


def _full(shape):
    zeros = (0,) * len(shape)
    return pl.BlockSpec(shape, lambda b, j: zeros)


def _project(kernel_fn, xc, mods_i, consts, tabs, heads, tile, n_lat_tiles, name):
    bsz, ttot, d = xc.shape
    n_tiles = ttot // tile
    nq, nk, nv = heads

    def mod_map(b, j):
        return (jnp.where(j < n_lat_tiles, b, bsz), 0, 0)

    in_specs = [pl.BlockSpec((1, tile, d), lambda b, j: (b, j, 0)),
                pl.BlockSpec((1, N_MOD, d), mod_map)]
    in_specs += [_full(a.shape) for a in consts]
    in_specs += [pl.BlockSpec((tile, LANE), lambda b, j: (j, 0))] * 3
    out_shape = [jax.ShapeDtypeStruct((bsz, n, ttot, LANE), MXU_DTYPE) for n in (nq, nk, nv)]
    out_specs = [pl.BlockSpec((1, n, tile, LANE), lambda b, j: (b, 0, j, 0)) for n in (nq, nk, nv)]
    return pl.pallas_call(
        kernel_fn, out_shape=out_shape, grid=(bsz, n_tiles), in_specs=in_specs, out_specs=out_specs,
        compiler_params=_params(("arbitrary", "arbitrary")), name=name,
    )(xc, mods_i, *consts, *tabs)


def _attn_kernel(q_ref, k_ref, v_ref, o_ref, *, t_lat, t_ctx, tk, nq_lat):
    qt = pl.program_id(2)
    q = q_ref[0, 0]
    tq = q.shape[0]

    def step(k, v, carry):
        m, l, acc = carry
        s = _dot_nt(q, k)
        m_new = jnp.maximum(m, jnp.max(s, axis=-1, keepdims=True))
        alpha = jnp.exp(m - m_new)
        p = jnp.exp(s - m_new)
        l = alpha * l + jnp.sum(p, axis=-1, keepdims=True)
        acc = alpha * acc + _dot(p.astype(v.dtype), v)
        return m_new, l, acc

    def body(c, carry):
        st = pl.multiple_of(c * tk, tk)
        return step(k_ref[0, 0, pl.ds(st, tk), :], v_ref[0, 0, pl.ds(st, tk), :], carry)

    init = (jnp.full((tq, 1), -1e30, F32), jnp.zeros((tq, 1), F32), jnp.zeros((tq, LANE), F32))
    n_lat_chunks = jnp.where(qt < nq_lat, t_lat // tk, 0)
    carry = lax.fori_loop(0, n_lat_chunks, body, init)
    _, l, acc = step(k_ref[0, 0, t_lat:t_lat + t_ctx, :], v_ref[0, 0, t_lat:t_lat + t_ctx, :], carry)
    o_ref[0, 0] = acc / l


def _attention(q, k, v, kmap, vmap, t_lat, t_ctx, tq, tk, name):
    bsz, nq, ttot, _ = q.shape
    kern = functools.partial(_attn_kernel, t_lat=t_lat, t_ctx=t_ctx, tk=tk, nq_lat=t_lat // tq)
    return pl.pallas_call(
        kern,
        out_shape=jax.ShapeDtypeStruct((bsz, nq, ttot, LANE), F32),
        grid=(bsz, nq, ttot // tq),
        in_specs=[pl.BlockSpec((1, 1, tq, LANE), lambda b, h, t: (b, h, t, 0)),
                  pl.BlockSpec((1, 1, ttot, LANE), lambda b, h, t: (b, kmap(h), 0, 0)),
                  pl.BlockSpec((1, 1, ttot, LANE), lambda b, h, t: (b, vmap(h), 0, 0))],
        out_specs=pl.BlockSpec((1, 1, tq, LANE), lambda b, h, t: (b, h, t, 0)),
        compiler_params=_params(("arbitrary", "arbitrary", "arbitrary")), name=name,
    )(q, k, v)


def _outproj_kernel(*refs, even, lam_init):
    if even:
        o_ref, x_ref, mod_ref, wout_ref, gffn_ref, wr_ref, lam_ref, subln_ref, xo_ref, h2_ref, aff_ref = refs
    else:
        o_ref, x_ref, mod_ref, wout_ref, gffn_ref, wr_ref, xo_ref, h2_ref, aff_ref = refs
    mod = mod_ref[0]
    if even:
        lv = lam_ref[...]
        lam = (jnp.exp(jnp.sum(lv[0:1] * lv[1:2], axis=-1, keepdims=True))
               - jnp.exp(jnp.sum(lv[2:3] * lv[3:4], axis=-1, keepdims=True)) + lam_init)
        parts = []
        for hh in range(4):
            oa = o_ref[0, 2 * hh] - lam * o_ref[0, 2 * hh + 1]
            oa = oa * lax.rsqrt(jnp.mean(oa * oa, axis=-1, keepdims=True) + EPS) * subln_ref[...]
            parts.append((oa * (1.0 - lam_init)).astype(MXU_DTYPE))
        for hh in range(8):
            parts.append(o_ref[0, 8 + hh].astype(MXU_DTYPE))
    else:
        parts = [o_ref[0, hh].astype(MXU_DTYPE) for hh in range(MLA_HEADS)]
    y = _dot(jnp.concatenate(parts, axis=1), wout_ref[...])
    xn = x_ref[0] + mod[2:3] * y
    xo_ref[0] = xn
    h2 = _norm_mod(xn, gffn_ref[...], mod[3:4], mod[4:5])
    h2_ref[0] = h2
    logits = _dot_nt(wr_ref[...], h2, precision=HIGHEST)
    e = jnp.exp(logits - jnp.max(logits, axis=0, keepdims=True))
    aff_ref[0] = e / jnp.sum(e, axis=0, keepdims=True)


def _outproj(o, xc, mods_i, consts, even, lam_init, tile, n_lat_tiles, name):
    bsz, ttot, d = xc.shape
    nq = o.shape[1]

    def mod_map(b, j):
        return (jnp.where(j < n_lat_tiles, b, bsz), 0, 0)

    in_specs = [pl.BlockSpec((1, nq, tile, LANE), lambda b, j: (b, 0, j, 0)),
                pl.BlockSpec((1, tile, d), lambda b, j: (b, j, 0)),
                pl.BlockSpec((1, N_MOD, d), mod_map)]
    in_specs += [_full(a.shape) for a in consts]
    out_shape = [jax.ShapeDtypeStruct((bsz, ttot, d), F32), jax.ShapeDtypeStruct((bsz, ttot, d), F32),
                 jax.ShapeDtypeStruct((bsz, N_EXPERTS, ttot), F32)]
    out_specs = [pl.BlockSpec((1, tile, d), lambda b, j: (b, j, 0)),
                 pl.BlockSpec((1, tile, d), lambda b, j: (b, j, 0)),
                 pl.BlockSpec((1, N_EXPERTS, tile), lambda b, j: (b, 0, j))]
    return pl.pallas_call(
        functools.partial(_outproj_kernel, even=even, lam_init=lam_init),
        out_shape=out_shape, grid=(bsz, ttot // tile), in_specs=in_specs, out_specs=out_specs,
        compiler_params=_params(("arbitrary", "arbitrary")), name=name,
    )(o, xc, mods_i, *consts)


def _cumsum_lanes(x, tri):
    bl = tri.shape[0]
    outs = []
    carry = jnp.zeros((x.shape[0], 1), F32)
    for c in range(x.shape[1] // bl):
        r = _dot(x[:, c * bl:(c + 1) * bl].astype(MXU_DTYPE), tri) + carry
        outs.append(r)
        carry = r[:, bl - 1:bl]
    return outs[0] if len(outs) == 1 else jnp.concatenate(outs, axis=1)


def _route_kernel(aff_ref, tri_ref, g_ref, pos_ref, idx_ref, cum_scr, *, cap, chunk):
    a = aff_ref[0]
    n_tok = a.shape[1]
    bits = pltpu.bitcast(a, I32)

    def search(i, cur):
        cand = cur | lax.shift_left(jnp.int32(1), 30 - i)
        cnt = jnp.sum((bits >= cand).astype(I32), axis=1, keepdims=True)
        return jnp.where(cnt >= cap, cand, cur)

    thr = lax.fori_loop(0, 31, search, jnp.zeros((N_EXPERTS, 1), I32))
    gt = bits > thr
    eq = bits == thr
    need = cap - jnp.sum(gt.astype(F32), axis=1, keepdims=True)
    eqf = eq.astype(F32)
    tri = tri_ref[...]
    eq_rank = _cumsum_lanes(eqf, tri) - eqf
    sel = jnp.logical_or(gt, jnp.logical_and(eq, eq_rank < need))
    self = sel.astype(F32)
    cum = _cumsum_lanes(self, tri)
    g_ref[0] = jnp.where(sel, a, 0.0)
    pos_ref[0] = (cum - self).astype(I32)

    n_chunks = n_tok // chunk
    for c in range(n_chunks):
        cum_scr[c] = cum[:, c * chunk:(c + 1) * chunk]
    slot = lax.broadcasted_iota(I32, (cap, 1), 0).astype(F32)
    lane = lax.broadcasted_iota(I32, (1, LANE), 1)
    idxmat = jnp.zeros((cap, LANE), F32)
    for e in range(N_EXPERTS):
        def count(c, acc):
            return acc + jnp.where(cum_scr[c, e:e + 1, :] <= slot, 1.0, 0.0)

        acc = lax.fori_loop(0, n_chunks, count, jnp.zeros((cap, chunk), F32))
        idxmat = idxmat + jnp.sum(acc, axis=1, keepdims=True) * (lane == e).astype(F32)
    idx_ref[0] = idxmat.astype(I32)


def _route(aff_t, tri, n_tok, blk_off, cap, name):
    bsz = aff_t.shape[0]
    chunk = min(512, n_tok)
    kern = functools.partial(_route_kernel, cap=cap, chunk=chunk)
    return pl.pallas_call(
        kern,
        out_shape=[jax.ShapeDtypeStruct((bsz, N_EXPERTS, n_tok), F32),
                   jax.ShapeDtypeStruct((bsz, N_EXPERTS, n_tok), I32),
                   jax.ShapeDtypeStruct((bsz, cap, LANE), I32)],
        grid=(bsz,),
        in_specs=[pl.BlockSpec((1, N_EXPERTS, n_tok), lambda b: (b, 0, blk_off)),
                  pl.BlockSpec(tri.shape, lambda b: (0, 0))],
        out_specs=[pl.BlockSpec((1, N_EXPERTS, n_tok), lambda b: (b, 0, 0)),
                   pl.BlockSpec((1, N_EXPERTS, n_tok), lambda b: (b, 0, 0)),
                   pl.BlockSpec((1, cap, LANE), lambda b: (b, 0, 0))],
        scratch_shapes=[pltpu.VMEM((n_tok // chunk, N_EXPERTS, chunk), F32)],
        compiler_params=_params(("arbitrary",)), name=name,
    )(aff_t, tri)


def _expert_kernel(idx_ref, h_hbm, wg_ref, wu_ref, wd_ref, y_ref, buf, sem):
    tm = buf.shape[0]

    def issue(r, carry):
        row = idx_ref[0, 0, r]
        pltpu.make_async_copy(h_hbm.at[pl.ds(row, 1), :], buf.at[pl.ds(r, 1), :], sem).start()
        return carry

    lax.fori_loop(0, tm, issue, 0)
    pltpu.make_async_copy(h_hbm.at[pl.ds(0, tm), :], buf, sem).wait()
    x = buf[...].astype(MXU_DTYPE)
    a = _dot(x, wg_ref[0, 0])
    u = _dot(x, wu_ref[0, 0])
    hmid = (a * jax.nn.sigmoid(a) * u).astype(MXU_DTYPE)
    y_ref[0, 0] = _dot(hmid, wd_ref[0, 0]).astype(y_ref.dtype)


def _experts(rows, h_flat, wg, wu, wd, layer, bsz, cap, tm, name):
    d = h_flat.shape[1]
    nt = cap // tm
    wspec = pl.BlockSpec((1, 1, d, d), lambda e, b, t: (layer, e, 0, 0))
    return pl.pallas_call(
        _expert_kernel,
        out_shape=jax.ShapeDtypeStruct((N_EXPERTS, bsz, cap, d), MXU_DTYPE),
        grid=(N_EXPERTS, bsz, nt),
        in_specs=[pl.BlockSpec((1, 1, tm), lambda e, b, t: ((e * bsz + b) * nt + t, 0, 0),
                               memory_space=pltpu.SMEM),
                  pl.BlockSpec(memory_space=pl.ANY), wspec, wspec, wspec],
        out_specs=pl.BlockSpec((1, 1, tm, d), lambda e, b, t: (e, b, t, 0)),
        scratch_shapes=[pltpu.VMEM((tm, d), F32), pltpu.SemaphoreType.DMA(())],
        compiler_params=_params(("arbitrary", "arbitrary", "arbitrary")), name=name,
    )(rows, h_flat, wg, wu, wd)


def _combine_kernel(a0_ref, x_ref, mod_ref, g_ref, pos_ref, ys_hbm, o_ref, buf, sem, *, w_dma, n_tiles):
    b = pl.program_id(0)
    j = pl.program_id(1)
    tile = x_ref.shape[1]
    kbuf = buf.shape[1]
    base = (b * n_tiles + j) * N_EXPERTS

    def window(e, slot):
        a0 = pl.multiple_of(a0_ref[base + e], BF16_SUBLANES)
        return pltpu.make_async_copy(ys_hbm.at[e, b, pl.ds(a0, w_dma), :], buf.at[slot, pl.ds(0, w_dma), :],
                                     sem.at[slot])

    window(0, 0).start()
    if w_dma < kbuf:
        for slot in range(2):
            buf[slot, w_dma:, :] = jnp.zeros((kbuf - w_dma, buf.shape[2]), buf.dtype)

    sub = lax.broadcasted_iota(I32, (N_EXPERTS, 1), 0)
    a0v = jnp.zeros((N_EXPERTS, 1), I32)
    for e in range(N_EXPERTS):
        a0v = jnp.where(sub == e, a0_ref[base + e], a0v)
    rel = jnp.clip(pos_ref[0] - a0v, -1, kbuf) + 1
    eye = (lax.broadcasted_iota(I32, (tile, tile), 0) == lax.broadcasted_iota(I32, (tile, tile), 1)
           ).astype(MXU_DTYPE)

    def to_cols(m):
        return _dot_nt(eye, m.astype(MXU_DTYPE))

    rel_col = 16.0 * to_cols(rel >> 4) + to_cols(rel & 15) - 1.0
    g = g_ref[0]
    g_hi = g.astype(MXU_DTYPE)
    r1 = g - g_hi.astype(F32)
    g_mid = r1.astype(MXU_DTYPE)
    g_lo = (r1 - g_mid.astype(F32)).astype(MXU_DTYPE)
    g_col = _dot_nt(eye, g_hi) + _dot_nt(eye, g_mid) + _dot_nt(eye, g_lo)

    lane = lax.broadcasted_iota(I32, (1, kbuf), 1).astype(F32)
    acc = jnp.zeros((tile, x_ref.shape[2]), F32)
    for e in range(N_EXPERTS):
        slot = e % 2
        if e + 1 < N_EXPERTS:
            window(e + 1, 1 - slot).start()
        window(e, slot).wait()
        onehot = jnp.where(rel_col[:, e:e + 1] == lane, 1.0, 0.0).astype(MXU_DTYPE)
        acc = acc + g_col[:, e:e + 1] * _dot(onehot, buf[slot])
    o_ref[0] = x_ref[0] + mod_ref[0][5:6] * acc


def _combine(a0, xc, mods_i, gates, pos, ys, tile, blk_off, mod_row, w_dma, name):
    bsz, _, d = xc.shape
    n_tok = gates.shape[2]
    n_tiles = n_tok // tile
    kbuf = 2 * LANE

    def mod_map(b, j, a0_ref):
        return (b if mod_row is None else mod_row, 0, 0)

    grid_spec = pltpu.PrefetchScalarGridSpec(
        num_scalar_prefetch=1, grid=(bsz, n_tiles),
        in_specs=[pl.BlockSpec((1, tile, d), lambda b, j, a: (b, blk_off + j, 0)),
                  pl.BlockSpec((1, N_MOD, d), mod_map),
                  pl.BlockSpec((1, N_EXPERTS, tile), lambda b, j, a: (b, 0, j)),
                  pl.BlockSpec((1, N_EXPERTS, tile), lambda b, j, a: (b, 0, j)),
                  pl.BlockSpec(memory_space=pl.ANY)],
        out_specs=pl.BlockSpec((1, tile, d), lambda b, j, a: (b, blk_off + j, 0)),
        scratch_shapes=[pltpu.VMEM((2, kbuf, d), MXU_DTYPE), pltpu.SemaphoreType.DMA((2,))])
    return pl.pallas_call(
        functools.partial(_combine_kernel, w_dma=w_dma, n_tiles=n_tiles),
        out_shape=jax.ShapeDtypeStruct(xc.shape, F32), grid_spec=grid_spec,
        input_output_aliases={1: 0},
        compiler_params=_params(("arbitrary", "arbitrary")), name=name,
    )(a0, xc, mods_i, gates, pos, ys)


def _rope_tables(t_lat, t_ctx, rot_dim, lane_off, period):
    half = rot_dim // 2
    quarter = rot_dim // 4
    t = np.arange(t_lat)
    freqs = ROPE_THETA ** (-np.arange(0, half, 2, dtype=np.float32) / half)
    ang_row = jnp.asarray((t // GRID_W).astype(np.float32))[:, None] * jnp.asarray(freqs)
    ang_col = jnp.asarray((t % GRID_W).astype(np.float32))[:, None] * jnp.asarray(freqs)
    lane = np.arange(LANE)
    u = (lane - lane_off) % period
    active = (lane >= lane_off) & (u < rot_dim)
    is_col = (u // half) == 1
    w = u % half
    fidx = w % quarter
    first = w < quarter
    ang = jnp.where(jnp.asarray(is_col)[None, :], ang_col[:, fidx], ang_row[:, fidx])
    act = jnp.asarray(active)[None, :]
    cos = jnp.where(act, jnp.cos(ang), 1.0)
    sin = jnp.where(act, jnp.sin(ang), 0.0)
    sin_up = jnp.where(jnp.asarray(~first)[None, :], sin, 0.0)
    sin_dn = jnp.where(jnp.asarray(first)[None, :], -sin, 0.0)
    pad = lambda a, v: jnp.concatenate([a, jnp.full((t_ctx, LANE), v, F32)], axis=0)
    return pad(cos.astype(F32), 1.0), pad(sin_up.astype(F32), 0.0), pad(sin_dn.astype(F32), 0.0)


def _lanes(vec, offset=0):
    return jnp.zeros((LANE,), F32).at[offset:offset + vec.shape[0]].set(vec)


def _pad_rows(rows):
    out = jnp.zeros((8, LANE), F32)
    return out.at[:len(rows)].set(jnp.stack(rows))


def _even_weights(w_in, w_out):
    d = w_in.shape[0]
    qb = w_in[:, 1536:2048].reshape(d, 8, HEAD_DIM)
    z = jnp.zeros_like(qb)
    g = (jnp.arange(8) // 4)[None, :, None]
    qb_pad = jnp.concatenate([jnp.where(g == 0, qb, z), jnp.where(g == 1, qb, z)], axis=-1).reshape(d, 8 * LANE)
    w1 = jnp.concatenate([w_in[:, :1536], qb_pad, w_in[:, 2048:]], axis=1).astype(MXU_DTYPE)
    ob = w_out[512:].reshape(8, HEAD_DIM, -1)
    zo = jnp.zeros_like(ob)
    go = (jnp.arange(8) // 4)[:, None, None]
    ob_pad = jnp.concatenate([jnp.where(go == 0, ob, zo), jnp.where(go == 1, ob, zo)], axis=1).reshape(8 * LANE, -1)
    wo = jnp.concatenate([w_out[:512], ob_pad], axis=0).astype(MXU_DTYPE)
    return w1, wo


def _odd_weights(w_in, w_q_up, w_kv_up, w_out):
    d = w_in.shape[0]
    nk = MLA_Q_RANK + MLA_KV_RANK
    w1 = jnp.zeros((d, nk + LANE), F32).at[:, :nk].set(w_in[:, :nk])
    w1 = w1.at[:, nk + MLA_NOPE:nk + MLA_NOPE + MLA_ROPE].set(w_in[:, nk:]).astype(MXU_DTYPE)
    dq = MLA_NOPE + MLA_ROPE
    wq = jnp.pad(w_q_up.reshape(MLA_Q_RANK, MLA_HEADS, dq), ((0, 0), (0, 0), (0, LANE - dq)))
    wq = wq.reshape(MLA_Q_RANK, MLA_HEADS * LANE).astype(MXU_DTYPE)
    kv = w_kv_up.reshape(MLA_KV_RANK, MLA_HEADS, 2 * MLA_NOPE)
    wk = jnp.pad(kv[:, :, :MLA_NOPE], ((0, 0), (0, 0), (0, LANE - MLA_NOPE)))
    wk = wk.reshape(MLA_KV_RANK, MLA_HEADS * LANE).astype(MXU_DTYPE)
    wv = kv[:, :, MLA_NOPE:].reshape(MLA_KV_RANK, MLA_HEADS * MLA_NOPE).astype(MXU_DTYPE)
    ob = w_out.reshape(MLA_HEADS, MLA_NOPE, -1)
    zo = jnp.zeros_like(ob)
    par = (jnp.arange(MLA_HEADS) % 2)[:, None, None]
    wo = jnp.concatenate([jnp.where(par == 0, ob, zo), jnp.where(par == 1, ob, zo)], axis=1)
    wo = wo.reshape(MLA_HEADS * LANE, -1).astype(MXU_DTYPE)
    return w1, wq, wk, wv, wo


def _segments(bounds):
    seg_id = np.zeros((LANE,), np.int32)
    for k, lo in enumerate(bounds):
        seg_id[lo:] = k
    return jnp.asarray(seg_id[:, None] == seg_id[None, :], dtype=MXU_DTYPE)


def _moe_set(xc, h_flat, aff_t, mods_i, weights, layer, *, n_tok, tok_off, tile_c, mod_row, tag):
    bsz, ttot, d = xc.shape
    wg, wu, wd = weights
    cap = EC_CAPACITY_FACTOR * n_tok // N_EXPERTS
    bl = min(2 * LANE, n_tok)
    tri = jnp.asarray(np.triu(np.ones((bl, bl), np.float32)), dtype=MXU_DTYPE)
    gates, pos, idxm = _route(aff_t, tri, n_tok, tok_off // n_tok, cap, "route_" + tag)
    idx = jnp.swapaxes(idxm[:, :, :N_EXPERTS], 1, 2)
    rows = idx + (jnp.arange(bsz, dtype=I32) * ttot + tok_off)[:, None, None]
    tm = min(256, cap)
    rows = jnp.swapaxes(rows, 0, 1).reshape(N_EXPERTS * bsz * (cap // tm), 1, tm)
    ys = _experts(rows, h_flat, wg, wu, wd, layer, bsz, cap, tm, "experts_" + tag)
    w_dma = min(cap, tile_c + 2 * BF16_SUBLANES)
    starts = pos[:, :, ::tile_c]
    a0 = jnp.minimum((starts // BF16_SUBLANES) * BF16_SUBLANES, cap - w_dma)
    a0 = jnp.swapaxes(a0, 1, 2).reshape(-1).astype(I32)
    return _combine(a0, xc, mods_i, gates, pos, ys, tile_c, tok_off // tile_c, mod_row, w_dma, "combine_" + tag)


def kernel(x, c, ctx, c_ctx, w_ada, b_ada, norm_mix, norm_ffn, w_in_even, a_qk_norm, diff_lambda, a_subln,
           b_qk_norm, w_out_even, w_in_odd, mla_q_norm, w_q_up, mla_kv_norm, w_kv_up, mla_qk_norm, w_out_odd,
           w_router, w_exp_gate, w_exp_up, w_exp_down):
    bsz, t_lat, d = x.shape
    t_ctx = ctx.shape[1]
    ttot = t_lat + t_ctx
    assert d == D_MODEL and bsz < MOD_ROWS and t_lat % t_ctx == 0 and t_lat % GRID_W == 0
    tile = min(256, t_ctx)
    tk = min(512, t_lat)
    n_lat_tiles = t_lat // tile

    xc = jnp.concatenate([x, ctx], axis=1)
    c_rows = jnp.zeros((MOD_ROWS, d), F32).at[:bsz].set(c).at[bsz].set(c_ctx)
    mods = _adaln(c_rows, w_ada, b_ada).reshape(DEPTH, MOD_ROWS, N_MOD, d)

    tabs_even = _rope_tables(t_lat, t_ctx, HEAD_DIM, 0, HEAD_DIM)
    tabs_odd = _rope_tables(t_lat, t_ctx, MLA_ROPE, MLA_NOPE, LANE)
    seg_even = _segments([0, HEAD_DIM])
    seg_odd = _segments([0, MLA_NOPE, MLA_NOPE + MLA_ROPE])
    invn_odd = jnp.concatenate([jnp.full((MLA_NOPE,), 1.0 / MLA_NOPE, F32),
                                jnp.full((LANE - MLA_NOPE,), 1.0 / MLA_ROPE, F32)])[None, :]
    expert_w = (w_exp_gate.astype(MXU_DTYPE), w_exp_up.astype(MXU_DTYPE), w_exp_down.astype(MXU_DTYPE))

    for i in range(DEPTH):
        last = i == DEPTH - 1
        j = i // 2
        mods_i = mods[i]
        g_mix = norm_mix[i][None, :]
        if i % 2 == 0:
            w1, wo = _even_weights(w_in_even[j], w_out_even[j])
            gains = _pad_rows([jnp.tile(a_qk_norm[j, 0], 2), jnp.tile(a_qk_norm[j, 1], 2),
                               jnp.tile(b_qk_norm[j, 0], 2), jnp.tile(b_qk_norm[j, 1], 2)])
            q, k, v = _project(_proj_even_kernel, xc, mods_i, [g_mix, w1, seg_even, gains], tabs_even,
                               (16, 5, 5), tile, n_lat_tiles, "proj_even")
            kmap = lambda h: jnp.where(h < 8, h // 2, 4)
            o = _attention(q, k, v, kmap, kmap, t_lat, t_ctx, tile, tk, "attn_even")
            lam_init = 0.8 - 0.6 * math.exp(-0.3 * i)
            extra = [diff_lambda[j], a_subln[j][None, :]]
        else:
            w1, wq, wk, wv, wo = _odd_weights(w_in_odd[j], w_q_up[j], w_kv_up[j], w_out_odd[j])
            qk = mla_qk_norm[j]
            gains = _pad_rows([_lanes(qk[0]), _lanes(qk[1, :MLA_NOPE]), _lanes(qk[1, MLA_NOPE:], MLA_NOPE)])
            consts = [g_mix, w1, mla_q_norm[j][None, :], wq, mla_kv_norm[j][None, :], wk, wv, seg_odd,
                      invn_odd, gains]
            q, k, v = _project(_proj_odd_kernel, xc, mods_i, consts, tabs_odd,
                               (MLA_HEADS, MLA_HEADS, MLA_HEADS // 2), tile, n_lat_tiles, "proj_odd")
            o = _attention(q, k, v, lambda h: h, lambda h: h // 2, t_lat, t_ctx, tile, tk, "attn_odd")
            lam_init = 0.0
            extra = []
        consts = [wo, norm_ffn[i][None, :], jnp.swapaxes(w_router[i], 0, 1)] + extra
        xc, h2, aff_t = _outproj(o, xc, mods_i, consts, i % 2 == 0, lam_init, tile, n_lat_tiles,
                                 "outproj_even" if i % 2 == 0 else "outproj_odd")
        h_flat = h2.reshape(bsz * ttot, d)
        xc = _moe_set(xc, h_flat, aff_t, mods_i, expert_w, i, n_tok=t_lat, tok_off=0,
                      tile_c=min(LANE, t_lat), mod_row=None, tag="lat")
        if not last:
            xc = _moe_set(xc, h_flat, aff_t, mods_i, expert_w, i, n_tok=t_ctx, tok_off=t_lat,
                          tile_c=t_ctx, mod_row=bsz, tag="ctx")
    return xc[:, :t_lat]
```

```python
import functools
import math

import numpy as np
import jax
import jax.numpy as jnp
from jax import lax
from jax.experimental import pallas as pl
from jax.experimental.pallas import tpu as pltpu

F32 = jnp.float32
I32 = jnp.int32
MXU_DTYPE = jnp.bfloat16

D_MODEL = 1024
DEPTH = 4
GRID_W = 64
HEAD_DIM = 64
ROPE_THETA = 10000.0
EPS = 1e-6
N_MOD = 6
N_EXPERTS = 16
EC_CAPACITY_FACTOR = 2
MLA_HEADS = 16
MLA_NOPE = 64
MLA_ROPE = 32
MLA_Q_RANK = 256
MLA_KV_RANK = 128

LANE = 128
BF16_SUBLANES = 16
MOD_ROWS = 16
VMEM_LIMIT = 52 * 1024 * 1024
HIGHEST = lax.Precision.HIGHEST
LOG2_E = math.log2(math.e)
MAX_SOFTMAX_REFERENCE = 56.0


def _dot(a, b):
    return jnp.dot(a, b, preferred_element_type=F32)


def _dot_nt(a, b, precision=None):
    return lax.dot_general(a, b, (((1,), (1,)), ((), ())), preferred_element_type=F32, precision=precision)


def _params(sem, vmem=VMEM_LIMIT):
    return pltpu.CompilerParams(dimension_semantics=sem, vmem_limit_bytes=vmem)


def _norm_mod(x, g, shift, scale):
    ms = jnp.mean(x * x, axis=-1, keepdims=True)
    return (x * lax.rsqrt(ms + EPS)) * g * (1.0 + scale) + shift


def _mod_kernel(c_ref, w_ref, b_ref, o_ref):
    c = c_ref[...]
    sc = c * jax.nn.sigmoid(c)
    o_ref[0] = jnp.dot(sc, w_ref[0], preferred_element_type=F32, precision=HIGHEST) + b_ref[0]


def _adaln(c_rows, w_ada, b_ada):
    depth, d, n = w_ada.shape
    nb = n // 4
    return pl.pallas_call(
        _mod_kernel,
        out_shape=jax.ShapeDtypeStruct((depth, MOD_ROWS, n), F32),
        grid=(depth, n // nb),
        in_specs=[pl.BlockSpec((MOD_ROWS, d), lambda i, j: (0, 0)),
                  pl.BlockSpec((1, d, nb), lambda i, j: (i, 0, j)),
                  pl.BlockSpec((1, 1, nb), lambda i, j: (i, 0, j))],
        out_specs=pl.BlockSpec((1, MOD_ROWS, nb), lambda i, j: (i, 0, j)),
        compiler_params=_params(("arbitrary", "arbitrary")),
        name="adaln",
    )(c_rows, w_ada, b_ada.reshape(depth, 1, n))


def _seg_normrope(blk, seg, inv_n, gain, tabs, dist):
    ss = _dot((blk * blk).astype(MXU_DTYPE), seg)
    xn = blk * lax.rsqrt(ss * inv_n + EPS) * gain
    if tabs is not None:
        cos, sin_up, sin_dn = tabs
        xn = xn * cos + pltpu.roll(xn, dist, 1) * sin_up + pltpu.roll(xn, LANE - dist, 1) * sin_dn
    return xn


def _proj_even_kernel(x_ref, mod_ref, g_ref, w_ref, seg_ref, gains_ref, cos_ref, sup_ref, sdn_ref,
                      q_ref, k_ref, v_ref):
    mod = mod_ref[0]
    h = _norm_mod(x_ref[0], g_ref[...], mod[0:1], mod[1:2]).astype(MXU_DTYPE)
    y = _dot(h, w_ref[...])
    seg = seg_ref[...]
    tabs = (cos_ref[...], sup_ref[...], sdn_ref[...])
    inv_n = 1.0 / HEAD_DIM
    scale = HEAD_DIM ** -0.5 * LOG2_E
    lane = lax.broadcasted_iota(I32, (1, LANE), 1)
    first = (lane < HEAD_DIM).astype(F32)
    second = 1.0 - first
    dist = HEAD_DIM // 4

    def blk(j):
        return y[:, LANE * j:LANE * (j + 1)]

    for hh in range(4):
        qa = _seg_normrope(blk(hh), seg, inv_n, gains_ref[0:1], tabs, dist) * scale
        q_ref[0, 2 * hh] = (qa * first).astype(q_ref.dtype)
        q_ref[0, 2 * hh + 1] = (qa * second).astype(q_ref.dtype)
        ka = _seg_normrope(blk(4 + hh), seg, inv_n, gains_ref[1:2], tabs, dist)
        k_ref[0, hh] = ka.astype(k_ref.dtype)
        v_ref[0, hh] = blk(8 + hh).astype(v_ref.dtype)
    for hh in range(8):
        qb = _seg_normrope(blk(12 + hh), seg, inv_n, gains_ref[2:3], tabs, dist) * scale
        q_ref[0, 8 + hh] = qb.astype(q_ref.dtype)
    kb = _seg_normrope(blk(20), seg, inv_n, gains_ref[3:4], tabs, dist)
    k_ref[0, 4] = kb.astype(k_ref.dtype)
    v_ref[0, 4] = blk(21).astype(v_ref.dtype)


def _proj_odd_kernel(x_ref, mod_ref, g_ref, w1_ref, qn_ref, wq_ref, kvn_ref, wk_ref, wv_ref, seg_ref,
                     invn_ref, gains_ref, cos_ref, sup_ref, sdn_ref, q_ref, k_ref, v_ref):
    mod = mod_ref[0]
    h = _norm_mod(x_ref[0], g_ref[...], mod[0:1], mod[1:2]).astype(MXU_DTYPE)
    y1 = _dot(h, w1_ref[...])
    cq = y1[:, :MLA_Q_RANK]
    cqn = (cq * lax.rsqrt(jnp.mean(cq * cq, axis=-1, keepdims=True) + EPS) * qn_ref[...]).astype(MXU_DTYPE)
    ckv = y1[:, MLA_Q_RANK:MLA_Q_RANK + MLA_KV_RANK]
    ckvn = (ckv * lax.rsqrt(jnp.mean(ckv * ckv, axis=-1, keepdims=True) + EPS) * kvn_ref[...]).astype(MXU_DTYPE)
    qf = _dot(cqn, wq_ref[...])
    kf = _dot(ckvn, wk_ref[...])
    vf = _dot(ckvn, wv_ref[...])
    seg = seg_ref[...]
    inv_n = invn_ref[...]
    tabs = (cos_ref[...], sup_ref[...], sdn_ref[...])
    scale = (MLA_NOPE + MLA_ROPE) ** -0.5 * LOG2_E
    dist = MLA_ROPE // 4
    kr = _seg_normrope(y1[:, MLA_Q_RANK + MLA_KV_RANK:], seg, inv_n, gains_ref[2:3], tabs, dist)
    for hh in range(MLA_HEADS):
        qh = _seg_normrope(qf[:, LANE * hh:LANE * (hh + 1)], seg, inv_n, gains_ref[0:1], tabs, dist) * scale
        q_ref[0, hh] = qh.astype(q_ref.dtype)
        kh = _seg_normrope(kf[:, LANE * hh:LANE * (hh + 1)], seg, inv_n, gains_ref[1:2], None, 0) + kr
        k_ref[0, hh] = kh.astype(k_ref.dtype)
    for j in range(MLA_HEADS // 2):
        v_ref[0, j] = vf[:, LANE * j:LANE * (j + 1)].astype(v_ref.dtype)


def _full(shape):
    zeros = (0,) * len(shape)
    return pl.BlockSpec(shape, lambda b, j: zeros)


def _project(kernel_fn, xc, mods_i, consts, tabs, heads, tile, n_lat_tiles, name):
    bsz, ttot, d = xc.shape
    n_tiles = ttot // tile
    nq, nk, nv = heads

    def mod_map(b, j):
        return (jnp.where(j < n_lat_tiles, b, bsz), 0, 0)

    in_specs = [pl.BlockSpec((1, tile, d), lambda b, j: (b, j, 0)),
                pl.BlockSpec((1, N_MOD, d), mod_map)]
    in_specs += [_full(a.shape) for a in consts]
    in_specs += [pl.BlockSpec((tile, LANE), lambda b, j: (j, 0))] * 3
    out_shape = [jax.ShapeDtypeStruct((bsz, n, ttot, LANE), MXU_DTYPE) for n in (nq, nk, nv)]
    out_specs = [pl.BlockSpec((1, n, tile, LANE), lambda b, j: (b, 0, j, 0)) for n in (nq, nk, nv)]
    return pl.pallas_call(
        kernel_fn, out_shape=out_shape, grid=(bsz, n_tiles), in_specs=in_specs, out_specs=out_specs,
        compiler_params=_params(("arbitrary", "arbitrary")), name=name,
    )(xc, mods_i, *consts, *tabs)


def _attn_kernel(q_ref, k_ref, v_ref, o_ref, knorm_scr, *, t_lat, t_ctx, tk, nq_lat):
    qt = pl.program_id(2)
    q = q_ref[0, 0]
    tq = q.shape[0]
    ttot = t_lat + t_ctx
    is_latent = qt < nq_lat

    @pl.when(qt == 0)
    def _():
        kmax2 = jnp.zeros((1, 1), F32)
        for lo in range(0, ttot, t_ctx):
            kf = k_ref[0, 0, lo:lo + t_ctx, :].astype(F32)
            r = jnp.sum(kf * kf, axis=-1, keepdims=True)
            kmax2 = jnp.maximum(kmax2, jnp.max(r, axis=0, keepdims=True))
        knorm_scr[...] = jnp.broadcast_to(jnp.sqrt(kmax2), knorm_scr.shape)

    qf = q.astype(F32)
    ref_row = jnp.sqrt(jnp.sum(qf * qf, axis=-1, keepdims=True)) * knorm_scr[0:1, 0:1]

    def kv(lo, size):
        return k_ref[0, 0, pl.ds(lo, size), :], v_ref[0, 0, pl.ds(lo, size), :]

    def fixed_reference(_):
        def step(lo, size, carry):
            l, acc = carry
            k, v = kv(lo, size)
            p = jnp.exp2(_dot_nt(q, k) - ref_row)
            for c in range(size // LANE):
                l = l + p[:, c * LANE:(c + 1) * LANE]
            return l, acc + _dot(p.astype(v.dtype), v)

        def latent_keys(carry):
            for c in range(t_lat // tk):
                carry = step(c * tk, tk, carry)
            return carry

        init = (jnp.zeros((tq, LANE), F32), jnp.zeros((tq, LANE), F32))
        l, acc = step(t_lat, t_ctx, lax.cond(is_latent, latent_keys, lambda cr: cr, init))
        return acc / jnp.sum(l, axis=-1, keepdims=True)

    def running_max(_):
        def step(lo, size, carry):
            m, l, acc = carry
            k, v = kv(lo, size)
            s = _dot_nt(q, k)
            m_new = jnp.maximum(m, jnp.max(s, axis=-1, keepdims=True))
            alpha = jnp.exp2(m - m_new)
            p = jnp.exp2(s - m_new)
            l = alpha * l + jnp.sum(p, axis=-1, keepdims=True)
            return m_new, l, alpha * acc + _dot(p.astype(v.dtype), v)

        init = (jnp.full((tq, 1), -1e30, F32), jnp.zeros((tq, 1), F32), jnp.zeros((tq, LANE), F32))
        n_chunks = jnp.where(is_latent, t_lat // tk, 0)
        carry = lax.fori_loop(0, n_chunks, lambda c, cr: step(pl.multiple_of(c * tk, tk), tk, cr), init)
        _, l, acc = step(t_lat, t_ctx, carry)
        return acc / l

    in_range = jnp.max(ref_row) < MAX_SOFTMAX_REFERENCE
    o_ref[0, 0] = lax.cond(in_range, fixed_reference, running_max, 0)


def _attention(q, k, v, kmap, vmap, t_lat, t_ctx, tq, tk, name):
    bsz, nq, ttot, _ = q.shape
    kern = functools.partial(_attn_kernel, t_lat=t_lat, t_ctx=t_ctx, tk=tk, nq_lat=t_lat // tq)
    return pl.pallas_call(
        kern,
        out_shape=jax.ShapeDtypeStruct((bsz, nq, ttot, LANE), F32),
        grid=(bsz, nq, ttot // tq),
        in_specs=[pl.BlockSpec((1, 1, tq, LANE), lambda b, h, t: (b, h, t, 0)),
                  pl.BlockSpec((1, 1, ttot, LANE), lambda b, h, t: (b, kmap(h), 0, 0)),
                  pl.BlockSpec((1, 1, ttot, LANE), lambda b, h, t: (b, vmap(h), 0, 0))],
        out_specs=pl.BlockSpec((1, 1, tq, LANE), lambda b, h, t: (b, h, t, 0)),
        scratch_shapes=[pltpu.VMEM((8, LANE), F32)],
        compiler_params=_params(("arbitrary", "arbitrary", "arbitrary")), name=name,
    )(q, k, v)


def _outproj_kernel(*refs, even, lam_init):
    if even:
        o_ref, x_ref, mod_ref, wout_ref, gffn_ref, wr_ref, lam_ref, subln_ref, xo_ref, h2_ref, aff_ref = refs
    else:
        o_ref, x_ref, mod_ref, wout_ref, gffn_ref, wr_ref, xo_ref, h2_ref, aff_ref = refs
    mod = mod_ref[0]
    if even:
        lv = lam_ref[...]
        lam = (jnp.exp(jnp.sum(lv[0:1] * lv[1:2], axis=-1, keepdims=True))
               - jnp.exp(jnp.sum(lv[2:3] * lv[3:4], axis=-1, keepdims=True)) + lam_init)
        parts = []
        for hh in range(4):
            oa = o_ref[0, 2 * hh] - lam * o_ref[0, 2 * hh + 1]
            oa = oa * lax.rsqrt(jnp.mean(oa * oa, axis=-1, keepdims=True) + EPS) * subln_ref[...]
            parts.append((oa * (1.0 - lam_init)).astype(MXU_DTYPE))
        for hh in range(8):
            parts.append(o_ref[0, 8 + hh].astype(MXU_DTYPE))
    else:
        parts = [o_ref[0, hh].astype(MXU_DTYPE) for hh in range(MLA_HEADS)]
    y = _dot(jnp.concatenate(parts, axis=1), wout_ref[...])
    xn = x_ref[0] + mod[2:3] * y
    xo_ref[0] = xn
    h2 = _norm_mod(xn, gffn_ref[...], mod[3:4], mod[4:5])
    h2_ref[0] = h2
    logits = _dot_nt(wr_ref[...], h2, precision=HIGHEST)
    e = jnp.exp(logits - jnp.max(logits, axis=0, keepdims=True))
    aff_ref[0] = e / jnp.sum(e, axis=0, keepdims=True)


def _outproj(o, xc, mods_i, consts, even, lam_init, tile, n_lat_tiles, name):
    bsz, ttot, d = xc.shape
    nq = o.shape[1]

    def mod_map(b, j):
        return (jnp.where(j < n_lat_tiles, b, bsz), 0, 0)

    in_specs = [pl.BlockSpec((1, nq, tile, LANE), lambda b, j: (b, 0, j, 0)),
                pl.BlockSpec((1, tile, d), lambda b, j: (b, j, 0)),
                pl.BlockSpec((1, N_MOD, d), mod_map)]
    in_specs += [_full(a.shape) for a in consts]
    out_shape = [jax.ShapeDtypeStruct((bsz, ttot, d), F32), jax.ShapeDtypeStruct((bsz, ttot, d), F32),
                 jax.ShapeDtypeStruct((bsz, N_EXPERTS, ttot), F32)]
    out_specs = [pl.BlockSpec((1, tile, d), lambda b, j: (b, j, 0)),
                 pl.BlockSpec((1, tile, d), lambda b, j: (b, j, 0)),
                 pl.BlockSpec((1, N_EXPERTS, tile), lambda b, j: (b, 0, j))]
    return pl.pallas_call(
        functools.partial(_outproj_kernel, even=even, lam_init=lam_init),
        out_shape=out_shape, grid=(bsz, ttot // tile), in_specs=in_specs, out_specs=out_specs,
        compiler_params=_params(("arbitrary", "arbitrary")), name=name,
    )(o, xc, mods_i, *consts)


def _cumsum_lanes(x, tri):
    bl = tri.shape[0]
    outs = []
    carry = jnp.zeros((x.shape[0], 1), F32)
    for c in range(x.shape[1] // bl):
        r = _dot(x[:, c * bl:(c + 1) * bl].astype(MXU_DTYPE), tri) + carry
        outs.append(r)
        carry = r[:, bl - 1:bl]
    return outs[0] if len(outs) == 1 else jnp.concatenate(outs, axis=1)


def _route_kernel(aff_ref, tri_ref, g_ref, pos_ref, idx_ref, cum_scr, *, cap, chunk):
    a = aff_ref[0]
    n_tok = a.shape[1]
    bits = pltpu.bitcast(a, I32)

    thr = _threshold_bits(bits, cap)
    gt = bits > thr
    eq = bits == thr
    need = cap - jnp.sum(gt.astype(F32), axis=1, keepdims=True)
    eqf = eq.astype(F32)
    tri = tri_ref[...]
    eq_rank = _cumsum_lanes(eqf, tri) - eqf
    sel = jnp.logical_or(gt, jnp.logical_and(eq, eq_rank < need))
    self = sel.astype(F32)
    cum = _cumsum_lanes(self, tri)
    g_ref[0] = jnp.where(sel, a, 0.0)
    pos_ref[0] = (cum - self).astype(I32)

    n_chunks = n_tok // chunk
    for c in range(n_chunks):
        cum_scr[c] = cum[:, c * chunk:(c + 1) * chunk]
    slot = lax.broadcasted_iota(I32, (cap, 1), 0).astype(F32)
    lane = lax.broadcasted_iota(I32, (1, LANE), 1)
    idxmat = jnp.zeros((cap, LANE), F32)
    for e in range(N_EXPERTS):
        def count(c, acc):
            return acc + jnp.where(cum_scr[c, e:e + 1, :] <= slot, 1.0, 0.0)

        acc = lax.fori_loop(0, n_chunks, count, jnp.zeros((cap, chunk), F32))
        idxmat = idxmat + jnp.sum(acc, axis=1, keepdims=True) * (lane == e).astype(F32)
    idx_ref[0] = idxmat.astype(I32)


def _threshold_bits(bits, cap):
    def search(i, cur):
        cand = cur | lax.shift_left(jnp.int32(1), 30 - i)
        cnt = jnp.sum((bits >= cand).astype(I32), axis=1, keepdims=True)
        return jnp.where(cnt >= cap, cand, cur)

    return lax.fori_loop(0, 31, search, jnp.zeros((bits.shape[0], 1), I32))


def _route_blocked_kernel(aff_ref, aff4_ref, tri_ref, low_ref, upp_ref, g_ref, pos_ref, idx_ref, *, cap):
    bits2 = pltpu.bitcast(aff_ref[0], I32)
    thr = _threshold_bits(bits2, cap)
    need = cap - jnp.sum((bits2 > thr).astype(F32), axis=1, keepdims=True)
    tri, low, upp = tri_ref[...], low_ref[...], upp_ref[...]
    nb = aff4_ref.shape[2]
    ones8 = jnp.ones((8, LANE), MXU_DTYPE)
    slot = lax.broadcasted_iota(I32, (cap, 1), 0).astype(F32)
    row_id = lax.broadcasted_iota(I32, (1, nb), 1).astype(F32)
    lane = lax.broadcasted_iota(I32, (1, LANE), 1)

    def prefix(x01):
        within = _dot(x01.astype(MXU_DTYPE), tri)
        tot = jnp.broadcast_to(within[:, LANE - 1:LANE], (nb, LANE)).astype(MXU_DTYPE)
        return within, _dot(low, tot)

    idxmat = jnp.zeros((cap, LANE), F32)
    for e in range(N_EXPERTS):
        a = aff4_ref[0, e]
        b = pltpu.bitcast(a, I32)
        t = thr[e:e + 1, :]
        gt = b > t
        eq = b == t
        eqf = eq.astype(F32)
        w_eq, off_eq = prefix(eqf)
        tie_ok = jnp.logical_and(eq, (w_eq + off_eq - eqf) < need[e:e + 1, :])
        sel = jnp.logical_or(gt, tie_ok)
        self = sel.astype(F32)
        within, rowoff = prefix(self)
        g_ref[0, e] = jnp.where(sel, a, 0.0)
        pos_ref[0, e] = (within + rowoff - self).astype(I32)
        tot_l = _dot_nt(ones8, self.astype(MXU_DTYPE))
        start_l = _dot(tot_l.astype(MXU_DTYPE), upp)[0:1]
        end_l = start_l + tot_l[0:1]
        inrow = jnp.where(start_l <= slot, jnp.where(slot < end_l, 1.0, 0.0), 0.0)
        local = _dot(inrow.astype(MXU_DTYPE), within.astype(MXU_DTYPE))
        rank = slot - jnp.sum(inrow * start_l, axis=1, keepdims=True)
        col = jnp.sum(jnp.where(local <= rank, 1.0, 0.0), axis=1, keepdims=True)
        row = jnp.sum(inrow * row_id, axis=1, keepdims=True)
        idxmat = idxmat + (row * LANE + col) * (lane == e).astype(F32)
    idx_ref[0] = idxmat.astype(I32)


def _route_blocked(aff_t, n_tok, tok_off, cap, name):
    bsz = aff_t.shape[0]
    nb = n_tok // LANE
    aff4 = aff_t[:, :, tok_off:tok_off + n_tok].reshape(bsz, N_EXPERTS, nb, LANE)
    tri = jnp.asarray(np.triu(np.ones((LANE, LANE), np.float32)), dtype=MXU_DTYPE)
    low = jnp.asarray(np.tril(np.ones((nb, nb), np.float32), -1), dtype=MXU_DTYPE)
    upp = jnp.asarray(np.triu(np.ones((nb, nb), np.float32), 1), dtype=MXU_DTYPE)
    blk4 = pl.BlockSpec((1, N_EXPERTS, nb, LANE), lambda b: (b, 0, 0, 0))
    g4, pos4, idxm = pl.pallas_call(
        functools.partial(_route_blocked_kernel, cap=cap),
        out_shape=[jax.ShapeDtypeStruct((bsz, N_EXPERTS, nb, LANE), F32),
                   jax.ShapeDtypeStruct((bsz, N_EXPERTS, nb, LANE), I32),
                   jax.ShapeDtypeStruct((bsz, cap, LANE), I32)],
        grid=(bsz,),
        in_specs=[pl.BlockSpec((1, N_EXPERTS, n_tok), lambda b: (b, 0, tok_off // n_tok)), blk4,
                  pl.BlockSpec(tri.shape, lambda b: (0, 0)), pl.BlockSpec(low.shape, lambda b: (0, 0)),
                  pl.BlockSpec(upp.shape, lambda b: (0, 0))],
        out_specs=[blk4, blk4, pl.BlockSpec((1, cap, LANE), lambda b: (b, 0, 0))],
        compiler_params=_params(("arbitrary",)), name=name,
    )(aff_t, aff4, tri, low, upp)
    return g4.reshape(bsz, N_EXPERTS, n_tok), pos4.reshape(bsz, N_EXPERTS, n_tok), idxm


def _route(aff_t, tri, n_tok, blk_off, cap, name):
    bsz = aff_t.shape[0]
    chunk = min(512, n_tok)
    kern = functools.partial(_route_kernel, cap=cap, chunk=chunk)
    return pl.pallas_call(
        kern,
        out_shape=[jax.ShapeDtypeStruct((bsz, N_EXPERTS, n_tok), F32),
                   jax.ShapeDtypeStruct((bsz, N_EXPERTS, n_tok), I32),
                   jax.ShapeDtypeStruct((bsz, cap, LANE), I32)],
        grid=(bsz,),
        in_specs=[pl.BlockSpec((1, N_EXPERTS, n_tok), lambda b: (b, 0, blk_off)),
                  pl.BlockSpec(tri.shape, lambda b: (0, 0))],
        out_specs=[pl.BlockSpec((1, N_EXPERTS, n_tok), lambda b: (b, 0, 0)),
                   pl.BlockSpec((1, N_EXPERTS, n_tok), lambda b: (b, 0, 0)),
                   pl.BlockSpec((1, cap, LANE), lambda b: (b, 0, 0))],
        scratch_shapes=[pltpu.VMEM((n_tok // chunk, N_EXPERTS, chunk), F32)],
        compiler_params=_params(("arbitrary",)), name=name,
    )(aff_t, tri)


def _expert_kernel(idx_ref, nxt_ref, h_hbm, wg_ref, wu_ref, wd_ref, y_ref, buf, sem):
    tm = buf.shape[1]
    step = (pl.program_id(0) * pl.num_programs(1) + pl.program_id(1)) * pl.num_programs(2) + pl.program_id(2)
    n_steps = pl.num_programs(0) * pl.num_programs(1) * pl.num_programs(2)
    cur = step % 2

    def gather(rows_ref, half):
        for r in range(tm):
            pltpu.make_async_copy(h_hbm.at[pl.ds(rows_ref[0, 0, r], 1), :], buf.at[half, pl.ds(r, 1), :],
                                  sem.at[half]).start()

    def wait(half):
        pltpu.make_async_copy(h_hbm.at[pl.ds(0, tm), :], buf.at[half], sem.at[half]).wait()

    @pl.when(step == 0)
    def _():
        gather(idx_ref, cur)

    gather(nxt_ref, 1 - cur)
    wait(cur)
    x = buf[cur].astype(MXU_DTYPE)
    a = _dot(x, wg_ref[0, 0])
    u = _dot(x, wu_ref[0, 0])
    hmid = (a * jax.nn.sigmoid(a) * u).astype(MXU_DTYPE)
    y_ref[0, 0] = _dot(hmid, wd_ref[0, 0]).astype(y_ref.dtype)

    @pl.when(step == n_steps - 1)
    def _():
        wait(1 - cur)


def _experts(rows, h_flat, wg, wu, wd, layer, bsz, cap, tm, name):
    d = h_flat.shape[1]
    nt = cap // tm
    last = N_EXPERTS * bsz * nt - 1
    wspec = pl.BlockSpec((1, 1, d, d), lambda e, b, t: (layer, e, 0, 0))

    def tile_id(e, b, t):
        return (e * bsz + b) * nt + t

    return pl.pallas_call(
        _expert_kernel,
        out_shape=jax.ShapeDtypeStruct((N_EXPERTS, bsz, cap, d), MXU_DTYPE),
        grid=(N_EXPERTS, bsz, nt),
        in_specs=[pl.BlockSpec((1, 1, tm), lambda e, b, t: (tile_id(e, b, t), 0, 0), memory_space=pltpu.SMEM),
                  pl.BlockSpec((1, 1, tm), lambda e, b, t: (jnp.minimum(tile_id(e, b, t) + 1, last), 0, 0),
                               memory_space=pltpu.SMEM),
                  pl.BlockSpec(memory_space=pl.ANY), wspec, wspec, wspec],
        out_specs=pl.BlockSpec((1, 1, tm, d), lambda e, b, t: (e, b, t, 0)),
        scratch_shapes=[pltpu.VMEM((2, tm, d), F32), pltpu.SemaphoreType.DMA((2,))],
        compiler_params=_params(("arbitrary", "arbitrary", "arbitrary")), name=name,
    )(rows, rows, h_flat, wg, wu, wd)


def _combine_kernel(a0_ref, x_ref, mod_ref, g_ref, pos_ref, ys_hbm, o_ref, buf, sem, *, w_dma, n_tiles):
    tile = x_ref.shape[1]
    kbuf = buf.shape[2]
    step = pl.program_id(0) * n_tiles + pl.program_id(1)
    n_steps = pl.num_programs(0) * n_tiles
    cur = step % 2
    base = step * N_EXPERTS

    def window(e, at_step, half):
        a0 = pl.multiple_of(a0_ref[at_step * N_EXPERTS + e], BF16_SUBLANES)
        return pltpu.make_async_copy(ys_hbm.at[e, at_step // n_tiles, pl.ds(a0, w_dma), :],
                                     buf.at[half, e, pl.ds(0, w_dma), :], sem.at[half])

    @pl.when(step == 0)
    def _():
        if w_dma < kbuf:
            buf[:, :, w_dma:, :] = jnp.zeros((2, N_EXPERTS, kbuf - w_dma, buf.shape[3]), buf.dtype)
        for e in range(N_EXPERTS):
            window(e, step, cur).start()

    @pl.when(step + 1 < n_steps)
    def _():
        for e in range(N_EXPERTS):
            window(e, step + 1, 1 - cur).start()

    sub = lax.broadcasted_iota(I32, (N_EXPERTS, 1), 0)
    a0v = jnp.zeros((N_EXPERTS, 1), I32)
    for e in range(N_EXPERTS):
        a0v = jnp.where(sub == e, a0_ref[base + e], a0v)
    rel = jnp.clip(pos_ref[0] - a0v, -1, kbuf) + 1
    eye = (lax.broadcasted_iota(I32, (tile, tile), 0) == lax.broadcasted_iota(I32, (tile, tile), 1)
           ).astype(MXU_DTYPE)

    def to_cols(m):
        return _dot_nt(eye, m.astype(MXU_DTYPE))

    rel_col = 16.0 * to_cols(rel >> 4) + to_cols(rel & 15) - 1.0
    g = g_ref[0]
    g_hi = g.astype(MXU_DTYPE)
    r1 = g - g_hi.astype(F32)
    g_mid = r1.astype(MXU_DTYPE)
    g_lo = (r1 - g_mid.astype(F32)).astype(MXU_DTYPE)
    g_col = _dot_nt(eye, g_hi) + _dot_nt(eye, g_mid) + _dot_nt(eye, g_lo)

    lane = lax.broadcasted_iota(I32, (1, kbuf), 1).astype(F32)
    acc = jnp.zeros((tile, x_ref.shape[2]), F32)
    pltpu.make_async_copy(ys_hbm.at[:, 0, pl.ds(0, w_dma), :], buf.at[cur, :, pl.ds(0, w_dma), :],
                          sem.at[cur]).wait()
    for e in range(N_EXPERTS):
        onehot = jnp.where(rel_col[:, e:e + 1] == lane, 1.0, 0.0).astype(MXU_DTYPE)
        acc = acc + g_col[:, e:e + 1] * _dot(onehot, buf[cur, e])
    o_ref[0] = x_ref[0] + mod_ref[0][5:6] * acc


def _combine(a0, xc, mods_i, gates, pos, ys, tile, blk_off, mod_row, w_dma, name):
    bsz, _, d = xc.shape
    n_tok = gates.shape[2]
    n_tiles = n_tok // tile
    kbuf = 2 * LANE

    def mod_map(b, j, a0_ref):
        return (b if mod_row is None else mod_row, 0, 0)

    grid_spec = pltpu.PrefetchScalarGridSpec(
        num_scalar_prefetch=1, grid=(bsz, n_tiles),
        in_specs=[pl.BlockSpec((1, tile, d), lambda b, j, a: (b, blk_off + j, 0)),
                  pl.BlockSpec((1, N_MOD, d), mod_map),
                  pl.BlockSpec((1, N_EXPERTS, tile), lambda b, j, a: (b, 0, j)),
                  pl.BlockSpec((1, N_EXPERTS, tile), lambda b, j, a: (b, 0, j)),
                  pl.BlockSpec(memory_space=pl.ANY)],
        out_specs=pl.BlockSpec((1, tile, d), lambda b, j, a: (b, blk_off + j, 0)),
        scratch_shapes=[pltpu.VMEM((2, N_EXPERTS, kbuf, d), MXU_DTYPE), pltpu.SemaphoreType.DMA((2,))])
    return pl.pallas_call(
        functools.partial(_combine_kernel, w_dma=w_dma, n_tiles=n_tiles),
        out_shape=jax.ShapeDtypeStruct(xc.shape, F32), grid_spec=grid_spec,
        input_output_aliases={1: 0},
        compiler_params=_params(("arbitrary", "arbitrary")), name=name,
    )(a0, xc, mods_i, gates, pos, ys)


def _rope_tables(t_lat, t_ctx, rot_dim, lane_off, period):
    half = rot_dim // 2
    quarter = rot_dim // 4
    t = np.arange(t_lat)
    freqs = ROPE_THETA ** (-np.arange(0, half, 2, dtype=np.float32) / half)
    ang_row = jnp.asarray((t // GRID_W).astype(np.float32))[:, None] * jnp.asarray(freqs)
    ang_col = jnp.asarray((t % GRID_W).astype(np.float32))[:, None] * jnp.asarray(freqs)
    lane = np.arange(LANE)
    u = (lane - lane_off) % period
    active = (lane >= lane_off) & (u < rot_dim)
    is_col = (u // half) == 1
    w = u % half
    fidx = w % quarter
    first = w < quarter
    ang = jnp.where(jnp.asarray(is_col)[None, :], ang_col[:, fidx], ang_row[:, fidx])
    act = jnp.asarray(active)[None, :]
    cos = jnp.where(act, jnp.cos(ang), 1.0)
    sin = jnp.where(act, jnp.sin(ang), 0.0)
    sin_up = jnp.where(jnp.asarray(~first)[None, :], sin, 0.0)
    sin_dn = jnp.where(jnp.asarray(first)[None, :], -sin, 0.0)
    pad = lambda a, v: jnp.concatenate([a, jnp.full((t_ctx, LANE), v, F32)], axis=0)
    return pad(cos.astype(F32), 1.0), pad(sin_up.astype(F32), 0.0), pad(sin_dn.astype(F32), 0.0)


def _lanes(vec, offset=0):
    return jnp.zeros((LANE,), F32).at[offset:offset + vec.shape[0]].set(vec)


def _pad_rows(rows):
    out = jnp.zeros((8, LANE), F32)
    return out.at[:len(rows)].set(jnp.stack(rows))


def _even_weights(w_in, w_out):
    d = w_in.shape[0]
    qb = w_in[:, 1536:2048].reshape(d, 8, HEAD_DIM)
    z = jnp.zeros_like(qb)
    g = (jnp.arange(8) // 4)[None, :, None]
    qb_pad = jnp.concatenate([jnp.where(g == 0, qb, z), jnp.where(g == 1, qb, z)], axis=-1).reshape(d, 8 * LANE)
    w1 = jnp.concatenate([w_in[:, :1536], qb_pad, w_in[:, 2048:]], axis=1).astype(MXU_DTYPE)
    ob = w_out[512:].reshape(8, HEAD_DIM, -1)
    zo = jnp.zeros_like(ob)
    go = (jnp.arange(8) // 4)[:, None, None]
    ob_pad = jnp.concatenate([jnp.where(go == 0, ob, zo), jnp.where(go == 1, ob, zo)], axis=1).reshape(8 * LANE, -1)
    wo = jnp.concatenate([w_out[:512], ob_pad], axis=0).astype(MXU_DTYPE)
    return w1, wo


def _odd_weights(w_in, w_q_up, w_kv_up, w_out):
    d = w_in.shape[0]
    nk = MLA_Q_RANK + MLA_KV_RANK
    w1 = jnp.zeros((d, nk + LANE), F32).at[:, :nk].set(w_in[:, :nk])
    w1 = w1.at[:, nk + MLA_NOPE:nk + MLA_NOPE + MLA_ROPE].set(w_in[:, nk:]).astype(MXU_DTYPE)
    dq = MLA_NOPE + MLA_ROPE
    wq = jnp.pad(w_q_up.reshape(MLA_Q_RANK, MLA_HEADS, dq), ((0, 0), (0, 0), (0, LANE - dq)))
    wq = wq.reshape(MLA_Q_RANK, MLA_HEADS * LANE).astype(MXU_DTYPE)
    kv = w_kv_up.reshape(MLA_KV_RANK, MLA_HEADS, 2 * MLA_NOPE)
    wk = jnp.pad(kv[:, :, :MLA_NOPE], ((0, 0), (0, 0), (0, LANE - MLA_NOPE)))
    wk = wk.reshape(MLA_KV_RANK, MLA_HEADS * LANE).astype(MXU_DTYPE)
    wv = kv[:, :, MLA_NOPE:].reshape(MLA_KV_RANK, MLA_HEADS * MLA_NOPE).astype(MXU_DTYPE)
    ob = w_out.reshape(MLA_HEADS, MLA_NOPE, -1)
    zo = jnp.zeros_like(ob)
    par = (jnp.arange(MLA_HEADS) % 2)[:, None, None]
    wo = jnp.concatenate([jnp.where(par == 0, ob, zo), jnp.where(par == 1, ob, zo)], axis=1)
    wo = wo.reshape(MLA_HEADS * LANE, -1).astype(MXU_DTYPE)
    return w1, wq, wk, wv, wo


def _segments(bounds):
    seg_id = np.zeros((LANE,), np.int32)
    for k, lo in enumerate(bounds):
        seg_id[lo:] = k
    return jnp.asarray(seg_id[:, None] == seg_id[None, :], dtype=MXU_DTYPE)


def _moe_set(xc, h_flat, aff_t, mods_i, weights, layer, *, n_tok, tok_off, tile_c, mod_row, tag):
    bsz, ttot, d = xc.shape
    wg, wu, wd = weights
    cap = EC_CAPACITY_FACTOR * n_tok // N_EXPERTS
    if n_tok % (8 * LANE) == 0:
        gates, pos, idxm = _route_blocked(aff_t, n_tok, tok_off, cap, "route_" + tag)
    else:
        bl = min(2 * LANE, n_tok)
        tri = jnp.asarray(np.triu(np.ones((bl, bl), np.float32)), dtype=MXU_DTYPE)
        gates, pos, idxm = _route(aff_t, tri, n_tok, tok_off // n_tok, cap, "route_" + tag)
    idx = jnp.swapaxes(idxm[:, :, :N_EXPERTS], 1, 2)
    rows = idx + (jnp.arange(bsz, dtype=I32) * ttot + tok_off)[:, None, None]
    tm = min(256, cap)
    rows = jnp.swapaxes(rows, 0, 1).reshape(N_EXPERTS * bsz * (cap // tm), 1, tm)
    ys = _experts(rows, h_flat, wg, wu, wd, layer, bsz, cap, tm, "experts_" + tag)
    w_dma = min(cap, tile_c + 2 * BF16_SUBLANES)
    starts = pos[:, :, ::tile_c]
    a0 = jnp.minimum((starts // BF16_SUBLANES) * BF16_SUBLANES, cap - w_dma)
    a0 = jnp.swapaxes(a0, 1, 2).reshape(-1).astype(I32)
    return _combine(a0, xc, mods_i, gates, pos, ys, tile_c, tok_off // tile_c, mod_row, w_dma, "combine_" + tag)


def kernel(x, c, ctx, c_ctx, w_ada, b_ada, norm_mix, norm_ffn, w_in_even, a_qk_norm, diff_lambda, a_subln,
           b_qk_norm, w_out_even, w_in_odd, mla_q_norm, w_q_up, mla_kv_norm, w_kv_up, mla_qk_norm, w_out_odd,
           w_router, w_exp_gate, w_exp_up, w_exp_down):
    bsz, t_lat, d = x.shape
    t_ctx = ctx.shape[1]
    ttot = t_lat + t_ctx
    assert d == D_MODEL and bsz < MOD_ROWS and t_lat % t_ctx == 0 and t_lat % GRID_W == 0
    tile = min(256, t_ctx)
    tk = min(512, t_lat)
    n_lat_tiles = t_lat // tile

    xc = jnp.concatenate([x, ctx], axis=1)
    c_rows = jnp.zeros((MOD_ROWS, d), F32).at[:bsz].set(c).at[bsz].set(c_ctx)
    mods = _adaln(c_rows, w_ada, b_ada).reshape(DEPTH, MOD_ROWS, N_MOD, d)

    tabs_even = _rope_tables(t_lat, t_ctx, HEAD_DIM, 0, HEAD_DIM)
    tabs_odd = _rope_tables(t_lat, t_ctx, MLA_ROPE, MLA_NOPE, LANE)
    seg_even = _segments([0, HEAD_DIM])
    seg_odd = _segments([0, MLA_NOPE, MLA_NOPE + MLA_ROPE])
    invn_odd = jnp.concatenate([jnp.full((MLA_NOPE,), 1.0 / MLA_NOPE, F32),
                                jnp.full((LANE - MLA_NOPE,), 1.0 / MLA_ROPE, F32)])[None, :]
    expert_w = (w_exp_gate.astype(MXU_DTYPE), w_exp_up.astype(MXU_DTYPE), w_exp_down.astype(MXU_DTYPE))

    for i in range(DEPTH):
        last = i == DEPTH - 1
        j = i // 2
        mods_i = mods[i]
        g_mix = norm_mix[i][None, :]
        if i % 2 == 0:
            w1, wo = _even_weights(w_in_even[j], w_out_even[j])
            gains = _pad_rows([jnp.tile(a_qk_norm[j, 0], 2), jnp.tile(a_qk_norm[j, 1], 2),
                               jnp.tile(b_qk_norm[j, 0], 2), jnp.tile(b_qk_norm[j, 1], 2)])
            q, k, v = _project(_proj_even_kernel, xc, mods_i, [g_mix, w1, seg_even, gains], tabs_even,
                               (16, 5, 5), tile, n_lat_tiles, "proj_even")
            kmap = lambda h: jnp.where(h < 8, h // 2, 4)
            o = _attention(q, k, v, kmap, kmap, t_lat, t_ctx, tile, tk, "attn_even")
            lam_init = 0.8 - 0.6 * math.exp(-0.3 * i)
            extra = [diff_lambda[j], a_subln[j][None, :]]
        else:
            w1, wq, wk, wv, wo = _odd_weights(w_in_odd[j], w_q_up[j], w_kv_up[j], w_out_odd[j])
            qk = mla_qk_norm[j]
            gains = _pad_rows([_lanes(qk[0]), _lanes(qk[1, :MLA_NOPE]), _lanes(qk[1, MLA_NOPE:], MLA_NOPE)])
            consts = [g_mix, w1, mla_q_norm[j][None, :], wq, mla_kv_norm[j][None, :], wk, wv, seg_odd,
                      invn_odd, gains]
            q, k, v = _project(_proj_odd_kernel, xc, mods_i, consts, tabs_odd,
                               (MLA_HEADS, MLA_HEADS, MLA_HEADS // 2), tile, n_lat_tiles, "proj_odd")
            o = _attention(q, k, v, lambda h: h, lambda h: h // 2, t_lat, t_ctx, tile, tk, "attn_odd")
            lam_init = 0.0
            extra = []
        consts = [wo, norm_ffn[i][None, :], jnp.swapaxes(w_router[i], 0, 1)] + extra
        xc, h2, aff_t = _outproj(o, xc, mods_i, consts, i % 2 == 0, lam_init, tile, n_lat_tiles,
                                 "outproj_even" if i % 2 == 0 else "outproj_odd")
        h_flat = h2.reshape(bsz * ttot, d)
        xc = _moe_set(xc, h_flat, aff_t, mods_i, expert_w, i, n_tok=t_lat, tok_off=0,
                      tile_c=min(LANE, t_lat), mod_row=None, tag="lat")
        if not last:
            xc = _moe_set(xc, h_flat, aff_t, mods_i, expert_w, i, n_tok=t_ctx, tok_off=t_lat,
                          tile_c=t_ctx, mod_row=bsz, tag="ctx")
    return xc[:, :t_lat]
```

```python
import functools
import math

import numpy as np
import jax
import jax.numpy as jnp
from jax import lax
from jax.experimental import pallas as pl
from jax.experimental.pallas import tpu as pltpu

F32 = jnp.float32
I32 = jnp.int32
MXU_DTYPE = jnp.bfloat16

D_MODEL = 1024
DEPTH = 4
GRID_W = 64
HEAD_DIM = 64
ROPE_THETA = 10000.0
EPS = 1e-6
N_MOD = 6
N_EXPERTS = 16
EC_CAPACITY_FACTOR = 2
MLA_HEADS = 16
MLA_NOPE = 64
MLA_ROPE = 32
MLA_Q_RANK = 256
MLA_KV_RANK = 128

LANE = 128
BF16_SUBLANES = 16
MOD_ROWS = 16
VMEM_LIMIT = 52 * 1024 * 1024
HIGHEST = lax.Precision.HIGHEST
LOG2_E = math.log2(math.e)
MAX_SOFTMAX_REFERENCE = 56.0


def _dot(a, b):
    return jnp.dot(a, b, preferred_element_type=F32)


def _dot_nt(a, b, precision=None):
    return lax.dot_general(a, b, (((1,), (1,)), ((), ())), preferred_element_type=F32, precision=precision)


def _split_bf16(x):
    hi = x.astype(MXU_DTYPE)
    return hi, (x - hi.astype(F32)).astype(MXU_DTYPE)


def _params(sem, vmem=VMEM_LIMIT):
    return pltpu.CompilerParams(dimension_semantics=sem, vmem_limit_bytes=vmem)


def _norm_mod(x, g, shift, scale):
    ms = jnp.mean(x * x, axis=-1, keepdims=True)
    return (x * lax.rsqrt(ms + EPS)) * g * (1.0 + scale) + shift


def _mod_kernel(c_ref, w_ref, b_ref, o_ref):
    c = c_ref[...]
    sc = c * jax.nn.sigmoid(c)
    o_ref[0] = jnp.dot(sc, w_ref[0], preferred_element_type=F32, precision=HIGHEST) + b_ref[0]


def _adaln(c_rows, w_ada, b_ada):
    depth, d, n = w_ada.shape
    nb = n // 4
    return pl.pallas_call(
        _mod_kernel,
        out_shape=jax.ShapeDtypeStruct((depth, MOD_ROWS, n), F32),
        grid=(depth, n // nb),
        in_specs=[pl.BlockSpec((MOD_ROWS, d), lambda i, j: (0, 0)),
                  pl.BlockSpec((1, d, nb), lambda i, j: (i, 0, j)),
                  pl.BlockSpec((1, 1, nb), lambda i, j: (i, 0, j))],
        out_specs=pl.BlockSpec((1, MOD_ROWS, nb), lambda i, j: (i, 0, j)),
        compiler_params=_params(("arbitrary", "arbitrary")),
        name="adaln",
    )(c_rows, w_ada, b_ada.reshape(depth, 1, n))


def _seg_normrope(blk, seg, inv_n, gain, tabs, dist):
    ss = _dot((blk * blk).astype(MXU_DTYPE), seg)
    xn = blk * lax.rsqrt(ss * inv_n + EPS) * gain
    if tabs is not None:
        cos, sin_up, sin_dn = tabs
        xn = xn * cos + pltpu.roll(xn, dist, 1) * sin_up + pltpu.roll(xn, LANE - dist, 1) * sin_dn
    return xn


def _proj_even_kernel(x_ref, mod_ref, g_ref, w_ref, seg_ref, gains_ref, cos_ref, sup_ref, sdn_ref,
                      q_ref, k_ref, v_ref):
    mod = mod_ref[0]
    h = _norm_mod(x_ref[0], g_ref[...], mod[0:1], mod[1:2]).astype(MXU_DTYPE)
    y = _dot(h, w_ref[...])
    seg = seg_ref[...]
    tabs = (cos_ref[...], sup_ref[...], sdn_ref[...])
    inv_n = 1.0 / HEAD_DIM
    scale = HEAD_DIM ** -0.5 * LOG2_E
    lane = lax.broadcasted_iota(I32, (1, LANE), 1)
    first = (lane < HEAD_DIM).astype(F32)
    second = 1.0 - first
    dist = HEAD_DIM // 4

    def blk(j):
        return y[:, LANE * j:LANE * (j + 1)]

    for hh in range(4):
        qa = _seg_normrope(blk(hh), seg, inv_n, gains_ref[0:1], tabs, dist) * scale
        q_ref[0, 2 * hh] = (qa * first).astype(q_ref.dtype)
        q_ref[0, 2 * hh + 1] = (qa * second).astype(q_ref.dtype)
        ka = _seg_normrope(blk(4 + hh), seg, inv_n, gains_ref[1:2], tabs, dist)
        k_ref[0, hh] = ka.astype(k_ref.dtype)
        v_ref[0, hh] = blk(8 + hh).astype(v_ref.dtype)
    for hh in range(8):
        qb = _seg_normrope(blk(12 + hh), seg, inv_n, gains_ref[2:3], tabs, dist) * scale
        q_ref[0, 8 + hh] = qb.astype(q_ref.dtype)
    kb = _seg_normrope(blk(20), seg, inv_n, gains_ref[3:4], tabs, dist)
    k_ref[0, 4] = kb.astype(k_ref.dtype)
    v_ref[0, 4] = blk(21).astype(v_ref.dtype)


def _proj_odd_kernel(x_ref, mod_ref, g_ref, w1_ref, qn_ref, wq_ref, kvn_ref, wk_ref, wv_ref, seg_ref,
                     invn_ref, gains_ref, cos_ref, sup_ref, sdn_ref, q_ref, k_ref, v_ref):
    mod = mod_ref[0]
    h = _norm_mod(x_ref[0], g_ref[...], mod[0:1], mod[1:2]).astype(MXU_DTYPE)
    y1 = _dot(h, w1_ref[...])
    cq = y1[:, :MLA_Q_RANK]
    cqn = (cq * lax.rsqrt(jnp.mean(cq * cq, axis=-1, keepdims=True) + EPS) * qn_ref[...]).astype(MXU_DTYPE)
    ckv = y1[:, MLA_Q_RANK:MLA_Q_RANK + MLA_KV_RANK]
    ckvn = (ckv * lax.rsqrt(jnp.mean(ckv * ckv, axis=-1, keepdims=True) + EPS) * kvn_ref[...]).astype(MXU_DTYPE)
    qf = _dot(cqn, wq_ref[...])
    kf = _dot(ckvn, wk_ref[...])
    vf = _dot(ckvn, wv_ref[...])
    seg = seg_ref[...]
    inv_n = invn_ref[...]
    tabs = (cos_ref[...], sup_ref[...], sdn_ref[...])
    scale = (MLA_NOPE + MLA_ROPE) ** -0.5 * LOG2_E
    dist = MLA_ROPE // 4
    kr = _seg_normrope(y1[:, MLA_Q_RANK + MLA_KV_RANK:], seg, inv_n, gains_ref[2:3], tabs, dist)
    for hh in range(MLA_HEADS):
        qh = _seg_normrope(qf[:, LANE * hh:LANE * (hh + 1)], seg, inv_n, gains_ref[0:1], tabs, dist) * scale
        q_ref[0, hh] = qh.astype(q_ref.dtype)
        kh = _seg_normrope(kf[:, LANE * hh:LANE * (hh + 1)], seg, inv_n, gains_ref[1:2], None, 0) + kr
        k_ref[0, hh] = kh.astype(k_ref.dtype)
    for j in range(MLA_HEADS // 2):
        v_ref[0, j] = vf[:, LANE * j:LANE * (j + 1)].astype(v_ref.dtype)


def _full(shape):
    zeros = (0,) * len(shape)
    return pl.BlockSpec(shape, lambda b, j: zeros)


def _project(kernel_fn, xc, mods_i, consts, tabs, heads, tile, n_lat_tiles, name):
    bsz, ttot, d = xc.shape
    n_tiles = ttot // tile
    nq, nk, nv = heads

    def mod_map(b, j):
        return (jnp.where(j < n_lat_tiles, b, bsz), 0, 0)

    in_specs = [pl.BlockSpec((1, tile, d), lambda b, j: (b, j, 0)),
                pl.BlockSpec((1, N_MOD, d), mod_map)]
    in_specs += [_full(a.shape) for a in consts]
    in_specs += [pl.BlockSpec((tile, LANE), lambda b, j: (j, 0))] * 3
    out_shape = [jax.ShapeDtypeStruct((bsz, n, ttot, LANE), MXU_DTYPE) for n in (nq, nk, nv)]
    out_specs = [pl.BlockSpec((1, n, tile, LANE), lambda b, j: (b, 0, j, 0)) for n in (nq, nk, nv)]
    return pl.pallas_call(
        kernel_fn, out_shape=out_shape, grid=(bsz, n_tiles), in_specs=in_specs, out_specs=out_specs,
        compiler_params=_params(("arbitrary", "arbitrary")), name=name,
    )(xc, mods_i, *consts, *tabs)


def _attn_kernel(q_ref, k_ref, v_ref, o_ref, knorm_scr, *, chunks):
    q = q_ref[0, 0]
    tq = q.shape[0]

    def kv(lo, size):
        return k_ref[0, 0, lo:lo + size, :], v_ref[0, 0, lo:lo + size, :]

    @pl.when(pl.program_id(2) == 0)
    def _():
        kmax2 = jnp.zeros((1, 1), F32)
        for lo, size in chunks:
            kf = kv(lo, size)[0].astype(F32)
            r = jnp.sum(kf * kf, axis=-1, keepdims=True)
            kmax2 = jnp.maximum(kmax2, jnp.max(r, axis=0, keepdims=True))
        knorm_scr[...] = jnp.broadcast_to(jnp.sqrt(kmax2), knorm_scr.shape)

    qf = q.astype(F32)
    ref_row = jnp.sqrt(jnp.sum(qf * qf, axis=-1, keepdims=True)) * knorm_scr[0:1, 0:1]

    def fixed_reference(_):
        l = jnp.zeros((tq, LANE), F32)
        acc = jnp.zeros((tq, LANE), F32)
        for lo, size in chunks:
            k, v = kv(lo, size)
            p = jnp.exp2(_dot_nt(q, k) - ref_row)
            for c in range(size // LANE):
                l = l + p[:, c * LANE:(c + 1) * LANE]
            acc = acc + _dot(p.astype(v.dtype), v)
        return acc / jnp.sum(l, axis=-1, keepdims=True)

    def running_max(_):
        def step(k, v, carry):
            m, l, acc = carry
            s = _dot_nt(q, k)
            m_new = jnp.maximum(m, jnp.max(s, axis=-1, keepdims=True))
            alpha = jnp.exp2(m - m_new)
            p = jnp.exp2(s - m_new)
            l = alpha * l + jnp.sum(p, axis=-1, keepdims=True)
            return m_new, l, alpha * acc + _dot(p.astype(v.dtype), v)

        carry = (jnp.full((tq, 1), -1e30, F32), jnp.zeros((tq, 1), F32), jnp.zeros((tq, LANE), F32))
        size0 = chunks[0][1]
        n_uniform = sum(1 for lo, size in chunks if size == size0 and lo % size0 == 0)

        def body(c, cr):
            lo = pl.multiple_of(c * size0, size0)
            return step(k_ref[0, 0, pl.ds(lo, size0), :], v_ref[0, 0, pl.ds(lo, size0), :], cr)

        carry = lax.fori_loop(0, n_uniform, body, carry)
        for lo, size in chunks[n_uniform:]:
            carry = step(*kv(lo, size), carry)
        _, l, acc = carry
        return acc / l

    in_range = jnp.max(ref_row) < MAX_SOFTMAX_REFERENCE
    o_ref[0, 0] = lax.cond(in_range, fixed_reference, running_max, 0)


def _attention(q, k, v, kmap, vmap, t_lat, t_ctx, tq, tk, name):
    bsz, nq, ttot, _ = q.shape
    chunks = [(c * tk, tk) for c in range(t_lat // tk)] + [(t_lat, t_ctx)]
    sem = ("arbitrary", "arbitrary", "arbitrary")
    scratch = [pltpu.VMEM((8, LANE), F32)]
    o_lat = pl.pallas_call(
        functools.partial(_attn_kernel, chunks=chunks),
        out_shape=jax.ShapeDtypeStruct((bsz, nq, t_lat, LANE), F32), grid=(bsz, nq, t_lat // tq),
        in_specs=[pl.BlockSpec((1, 1, tq, LANE), lambda b, h, t: (b, h, t, 0)),
                  pl.BlockSpec((1, 1, ttot, LANE), lambda b, h, t: (b, kmap(h), 0, 0)),
                  pl.BlockSpec((1, 1, ttot, LANE), lambda b, h, t: (b, vmap(h), 0, 0))],
        out_specs=pl.BlockSpec((1, 1, tq, LANE), lambda b, h, t: (b, h, t, 0)),
        scratch_shapes=scratch, compiler_params=_params(sem), name=name,
    )(q, k, v)
    ctx_blk = t_lat // t_ctx
    o_ctx = pl.pallas_call(
        functools.partial(_attn_kernel, chunks=[(0, t_ctx)]),
        out_shape=jax.ShapeDtypeStruct((bsz, nq, t_ctx, LANE), F32), grid=(bsz, nq, 1),
        in_specs=[pl.BlockSpec((1, 1, t_ctx, LANE), lambda b, h, t: (b, h, ctx_blk, 0)),
                  pl.BlockSpec((1, 1, t_ctx, LANE), lambda b, h, t: (b, kmap(h), ctx_blk, 0)),
                  pl.BlockSpec((1, 1, t_ctx, LANE), lambda b, h, t: (b, vmap(h), ctx_blk, 0))],
        out_specs=pl.BlockSpec((1, 1, t_ctx, LANE), lambda b, h, t: (b, h, 0, 0)),
        scratch_shapes=scratch, compiler_params=_params(sem), name=name + "_ctx",
    )(q, k, v)
    return o_lat, o_ctx


def _outproj_kernel(*refs, even, lam_init, n_lat_tiles):
    if even:
        (ol_ref, oc_ref, x_ref, mod_ref, wout_ref, gffn_ref, wr_ref, lam_ref, subln_ref,
         xo_ref, h2_ref, aff_ref) = refs
    else:
        ol_ref, oc_ref, x_ref, mod_ref, wout_ref, gffn_ref, wr_ref, xo_ref, h2_ref, aff_ref = refs
    mod = mod_ref[0]
    is_ctx = pl.program_id(1) >= n_lat_tiles

    def head(hh):
        return jnp.where(is_ctx, oc_ref[0, hh], ol_ref[0, hh])

    if even:
        lv = lam_ref[...]
        lam = (jnp.exp(jnp.sum(lv[0:1] * lv[1:2], axis=-1, keepdims=True))
               - jnp.exp(jnp.sum(lv[2:3] * lv[3:4], axis=-1, keepdims=True)) + lam_init)
        parts = []
        for hh in range(4):
            oa = head(2 * hh) - lam * head(2 * hh + 1)
            oa = oa * lax.rsqrt(jnp.mean(oa * oa, axis=-1, keepdims=True) + EPS) * subln_ref[...]
            parts.append((oa * (1.0 - lam_init)).astype(MXU_DTYPE))
        for hh in range(8):
            parts.append(head(8 + hh).astype(MXU_DTYPE))
    else:
        parts = [head(hh).astype(MXU_DTYPE) for hh in range(MLA_HEADS)]
    y = _dot(jnp.concatenate(parts, axis=1), wout_ref[...])
    xn = x_ref[0] + mod[2:3] * y
    xo_ref[0] = xn
    h2 = _norm_mod(xn, gffn_ref[...], mod[3:4], mod[4:5])
    h2_ref[0] = h2
    w_hi, w_lo = _split_bf16(wr_ref[...])
    h_hi, h_lo = _split_bf16(h2)
    logits = _dot_nt(w_hi, h_hi) + (_dot_nt(w_hi, h_lo) + _dot_nt(w_lo, h_hi))
    e = jnp.exp(logits - jnp.max(logits, axis=0, keepdims=True))
    aff_ref[0] = e / jnp.sum(e, axis=0, keepdims=True)


def _outproj(o_lat, o_ctx, xc, mods_i, consts, even, lam_init, tile, n_lat_tiles, name):
    bsz, ttot, d = xc.shape
    nq = o_lat.shape[1]

    def mod_map(b, j):
        return (jnp.where(j < n_lat_tiles, b, bsz), 0, 0)

    in_specs = [pl.BlockSpec((1, nq, tile, LANE), lambda b, j: (b, 0, jnp.minimum(j, n_lat_tiles - 1), 0)),
                pl.BlockSpec((1, nq, tile, LANE), lambda b, j: (b, 0, jnp.maximum(j - n_lat_tiles, 0), 0)),
                pl.BlockSpec((1, tile, d), lambda b, j: (b, j, 0)),
                pl.BlockSpec((1, N_MOD, d), mod_map)]
    in_specs += [_full(a.shape) for a in consts]
    out_shape = [jax.ShapeDtypeStruct((bsz, ttot, d), F32), jax.ShapeDtypeStruct((bsz, ttot, d), F32),
                 jax.ShapeDtypeStruct((bsz, N_EXPERTS, ttot), F32)]
    out_specs = [pl.BlockSpec((1, tile, d), lambda b, j: (b, j, 0)),
                 pl.BlockSpec((1, tile, d), lambda b, j: (b, j, 0)),
                 pl.BlockSpec((1, N_EXPERTS, tile), lambda b, j: (b, 0, j))]
    return pl.pallas_call(
        functools.partial(_outproj_kernel, even=even, lam_init=lam_init, n_lat_tiles=n_lat_tiles),
        out_shape=out_shape, grid=(bsz, ttot // tile), in_specs=in_specs, out_specs=out_specs,
        compiler_params=_params(("arbitrary", "arbitrary")), name=name,
    )(o_lat, o_ctx, xc, mods_i, *consts)


def _cumsum_lanes(x, tri):
    bl = tri.shape[0]
    outs = []
    carry = jnp.zeros((x.shape[0], 1), F32)
    for c in range(x.shape[1] // bl):
        r = _dot(x[:, c * bl:(c + 1) * bl].astype(MXU_DTYPE), tri) + carry
        outs.append(r)
        carry = r[:, bl - 1:bl]
    return outs[0] if len(outs) == 1 else jnp.concatenate(outs, axis=1)


def _route_kernel(aff_ref, tri_ref, g_ref, pos_ref, idx_ref, cum_scr, *, cap, chunk):
    a = aff_ref[0]
    n_tok = a.shape[1]
    bits = pltpu.bitcast(a, I32)

    thr = _threshold_bits(bits, cap)
    gt = bits > thr
    eq = bits == thr
    need = cap - jnp.sum(gt.astype(F32), axis=1, keepdims=True)
    eqf = eq.astype(F32)
    tri = tri_ref[...]
    eq_rank = _cumsum_lanes(eqf, tri) - eqf
    sel = jnp.logical_or(gt, jnp.logical_and(eq, eq_rank < need))
    self = sel.astype(F32)
    cum = _cumsum_lanes(self, tri)
    g_ref[0] = jnp.where(sel, a, 0.0)
    pos_ref[0] = (cum - self).astype(I32)

    n_chunks = n_tok // chunk
    for c in range(n_chunks):
        cum_scr[c] = cum[:, c * chunk:(c + 1) * chunk]
    slot = lax.broadcasted_iota(I32, (cap, 1), 0).astype(F32)
    lane = lax.broadcasted_iota(I32, (1, LANE), 1)
    idxmat = jnp.zeros((cap, LANE), F32)
    for e in range(N_EXPERTS):
        def count(c, acc):
            return acc + jnp.where(cum_scr[c, e:e + 1, :] <= slot, 1.0, 0.0)

        acc = lax.fori_loop(0, n_chunks, count, jnp.zeros((cap, chunk), F32))
        idxmat = idxmat + jnp.sum(acc, axis=1, keepdims=True) * (lane == e).astype(F32)
    idx_ref[0] = idxmat.astype(I32)


def _threshold_bits(bits, cap):
    def search(i, cur):
        cand = cur | lax.shift_left(jnp.int32(1), 30 - i)
        cnt = jnp.sum((bits >= cand).astype(I32), axis=1, keepdims=True)
        return jnp.where(cnt >= cap, cand, cur)

    return lax.fori_loop(0, 31, search, jnp.zeros((bits.shape[0], 1), I32))


def _route_blocked_kernel(aff_ref, aff4_ref, tri_ref, low_ref, upp_ref, g_ref, pos_ref, idx_ref, *, cap):
    bits2 = pltpu.bitcast(aff_ref[0], I32)
    thr = _threshold_bits(bits2, cap)
    need = cap - jnp.sum((bits2 > thr).astype(F32), axis=1, keepdims=True)
    tri, low, upp = tri_ref[...], low_ref[...], upp_ref[...]
    nb = aff4_ref.shape[2]
    ones8 = jnp.ones((8, LANE), MXU_DTYPE)
    slot = lax.broadcasted_iota(I32, (cap, 1), 0).astype(F32)
    row_id = lax.broadcasted_iota(I32, (1, nb), 1).astype(F32)
    lane = lax.broadcasted_iota(I32, (1, LANE), 1)

    def prefix(x01):
        within = _dot(x01.astype(MXU_DTYPE), tri)
        tot = jnp.broadcast_to(within[:, LANE - 1:LANE], (nb, LANE)).astype(MXU_DTYPE)
        return within, _dot(low, tot)

    idxmat = jnp.zeros((cap, LANE), F32)
    for e in range(N_EXPERTS):
        a = aff4_ref[0, e]
        b = pltpu.bitcast(a, I32)
        t = thr[e:e + 1, :]
        gt = b > t
        eq = b == t
        eqf = eq.astype(F32)
        w_eq, off_eq = prefix(eqf)
        tie_ok = jnp.logical_and(eq, (w_eq + off_eq - eqf) < need[e:e + 1, :])
        sel = jnp.logical_or(gt, tie_ok)
        self = sel.astype(F32)
        within, rowoff = prefix(self)
        g_ref[0, e] = jnp.where(sel, a, 0.0)
        pos_ref[0, e] = (within + rowoff - self).astype(I32)
        tot_l = _dot_nt(ones8, self.astype(MXU_DTYPE))
        start_l = _dot(tot_l.astype(MXU_DTYPE), upp)[0:1]
        end_l = start_l + tot_l[0:1]
        inrow = jnp.where(start_l <= slot, jnp.where(slot < end_l, 1.0, 0.0), 0.0)
        local = _dot(inrow.astype(MXU_DTYPE), within.astype(MXU_DTYPE))
        rank = slot - jnp.sum(inrow * start_l, axis=1, keepdims=True)
        col = jnp.sum(jnp.where(local <= rank, 1.0, 0.0), axis=1, keepdims=True)
        row = jnp.sum(inrow * row_id, axis=1, keepdims=True)
        idxmat = idxmat + (row * LANE + col) * (lane == e).astype(F32)
    idx_ref[0] = idxmat.astype(I32)


def _route_blocked(aff_t, n_tok, tok_off, cap, name):
    bsz = aff_t.shape[0]
    nb = n_tok // LANE
    aff4 = aff_t[:, :, tok_off:tok_off + n_tok].reshape(bsz, N_EXPERTS, nb, LANE)
    tri = jnp.asarray(np.triu(np.ones((LANE, LANE), np.float32)), dtype=MXU_DTYPE)
    low = jnp.asarray(np.tril(np.ones((nb, nb), np.float32), -1), dtype=MXU_DTYPE)
    upp = jnp.asarray(np.triu(np.ones((nb, nb), np.float32), 1), dtype=MXU_DTYPE)
    blk4 = pl.BlockSpec((1, N_EXPERTS, nb, LANE), lambda b: (b, 0, 0, 0))
    g4, pos4, idxm = pl.pallas_call(
        functools.partial(_route_blocked_kernel, cap=cap),
        out_shape=[jax.ShapeDtypeStruct((bsz, N_EXPERTS, nb, LANE), F32),
                   jax.ShapeDtypeStruct((bsz, N_EXPERTS, nb, LANE), I32),
                   jax.ShapeDtypeStruct((bsz, cap, LANE), I32)],
        grid=(bsz,),
        in_specs=[pl.BlockSpec((1, N_EXPERTS, n_tok), lambda b: (b, 0, tok_off // n_tok)), blk4,
                  pl.BlockSpec(tri.shape, lambda b: (0, 0)), pl.BlockSpec(low.shape, lambda b: (0, 0)),
                  pl.BlockSpec(upp.shape, lambda b: (0, 0))],
        out_specs=[blk4, blk4, pl.BlockSpec((1, cap, LANE), lambda b: (b, 0, 0))],
        compiler_params=_params(("arbitrary",)), name=name,
    )(aff_t, aff4, tri, low, upp)
    return g4.reshape(bsz, N_EXPERTS, n_tok), pos4.reshape(bsz, N_EXPERTS, n_tok), idxm


def _route(aff_t, tri, n_tok, blk_off, cap, name):
    bsz = aff_t.shape[0]
    chunk = min(512, n_tok)
    kern = functools.partial(_route_kernel, cap=cap, chunk=chunk)
    return pl.pallas_call(
        kern,
        out_shape=[jax.ShapeDtypeStruct((bsz, N_EXPERTS, n_tok), F32),
                   jax.ShapeDtypeStruct((bsz, N_EXPERTS, n_tok), I32),
                   jax.ShapeDtypeStruct((bsz, cap, LANE), I32)],
        grid=(bsz,),
        in_specs=[pl.BlockSpec((1, N_EXPERTS, n_tok), lambda b: (b, 0, blk_off)),
                  pl.BlockSpec(tri.shape, lambda b: (0, 0))],
        out_specs=[pl.BlockSpec((1, N_EXPERTS, n_tok), lambda b: (b, 0, 0)),
                   pl.BlockSpec((1, N_EXPERTS, n_tok), lambda b: (b, 0, 0)),
                   pl.BlockSpec((1, cap, LANE), lambda b: (b, 0, 0))],
        scratch_shapes=[pltpu.VMEM((n_tok // chunk, N_EXPERTS, chunk), F32)],
        compiler_params=_params(("arbitrary",)), name=name,
    )(aff_t, tri)


def _expert_kernel(idx_ref, nxt_ref, h_hbm, wg_ref, wu_ref, wd_ref, y_ref, buf, x_scr, sem):
    tm = buf.shape[1]
    step = (pl.program_id(0) * pl.num_programs(1) + pl.program_id(1)) * pl.num_programs(2) + pl.program_id(2)
    n_steps = pl.num_programs(0) * pl.num_programs(1) * pl.num_programs(2)
    cur = step % 2

    def gather(rows_ref, half):
        for r in range(tm):
            pltpu.make_async_copy(h_hbm.at[pl.ds(rows_ref[0, 0, r], 1), :], buf.at[half, pl.ds(r, 1), :],
                                  sem.at[half]).start()

    def wait(half):
        pltpu.make_async_copy(h_hbm.at[pl.ds(0, tm), :], buf.at[half], sem.at[half]).wait()

    @pl.when(step == 0)
    def _():
        gather(idx_ref, cur)

    wait(cur)
    x_scr[...] = buf[cur].astype(MXU_DTYPE)
    gather(nxt_ref, 1 - cur)
    x = x_scr[...]
    a = _dot(x, wg_ref[0, 0])
    u = _dot(x, wu_ref[0, 0])
    hmid = (a * jax.nn.sigmoid(a) * u).astype(MXU_DTYPE)
    y_ref[0, 0] = _dot(hmid, wd_ref[0, 0]).astype(y_ref.dtype)

    @pl.when(step == n_steps - 1)
    def _():
        wait(1 - cur)


def _experts(rows, h_flat, wg, wu, wd, layer, bsz, cap, tm, name):
    d = h_flat.shape[1]
    nt = cap // tm
    last = N_EXPERTS * bsz * nt - 1
    wspec = pl.BlockSpec((1, 1, d, d), lambda e, b, t: (layer, e, 0, 0))

    def tile_id(e, b, t):
        return (e * bsz + b) * nt + t

    return pl.pallas_call(
        _expert_kernel,
        out_shape=jax.ShapeDtypeStruct((N_EXPERTS, bsz, cap, d), MXU_DTYPE),
        grid=(N_EXPERTS, bsz, nt),
        in_specs=[pl.BlockSpec((1, 1, tm), lambda e, b, t: (tile_id(e, b, t), 0, 0), memory_space=pltpu.SMEM),
                  pl.BlockSpec((1, 1, tm), lambda e, b, t: (jnp.minimum(tile_id(e, b, t) + 1, last), 0, 0),
                               memory_space=pltpu.SMEM),
                  pl.BlockSpec(memory_space=pl.ANY), wspec, wspec, wspec],
        out_specs=pl.BlockSpec((1, 1, tm, d), lambda e, b, t: (e, b, t, 0)),
        scratch_shapes=[pltpu.VMEM((2, tm, d), F32), pltpu.VMEM((tm, d), MXU_DTYPE),
                        pltpu.SemaphoreType.DMA((2,))],
        compiler_params=_params(("arbitrary", "arbitrary", "arbitrary")), name=name,
    )(rows, rows, h_flat, wg, wu, wd)


def _combine_kernel(a0_ref, x_ref, mod_ref, g_ref, pos_ref, ys_hbm, o_ref, buf, sem, *, w_dma, n_tiles):
    tile = x_ref.shape[1]
    kbuf = buf.shape[2]
    step = pl.program_id(0) * n_tiles + pl.program_id(1)
    n_steps = pl.num_programs(0) * n_tiles
    cur = step % 2
    base = step * N_EXPERTS

    def window(e, at_step, half):
        a0 = pl.multiple_of(a0_ref[at_step * N_EXPERTS + e], BF16_SUBLANES)
        return pltpu.make_async_copy(ys_hbm.at[e, at_step // n_tiles, pl.ds(a0, w_dma), :],
                                     buf.at[half, e, pl.ds(0, w_dma), :], sem.at[half])

    @pl.when(step == 0)
    def _():
        if w_dma < kbuf:
            buf[:, :, w_dma:, :] = jnp.zeros((2, N_EXPERTS, kbuf - w_dma, buf.shape[3]), buf.dtype)
        for e in range(N_EXPERTS):
            window(e, step, cur).start()

    @pl.when(step + 1 < n_steps)
    def _():
        for e in range(N_EXPERTS):
            window(e, step + 1, 1 - cur).start()

    sub = lax.broadcasted_iota(I32, (N_EXPERTS, 1), 0)
    a0v = jnp.zeros((N_EXPERTS, 1), I32)
    for e in range(N_EXPERTS):
        a0v = jnp.where(sub == e, a0_ref[base + e], a0v)
    rel = jnp.clip(pos_ref[0] - a0v, -1, kbuf) + 1
    eye = (lax.broadcasted_iota(I32, (tile, tile), 0) == lax.broadcasted_iota(I32, (tile, tile), 1)
           ).astype(MXU_DTYPE)

    def to_cols(m):
        return _dot_nt(eye, m.astype(MXU_DTYPE))

    rel_col = 16.0 * to_cols(rel >> 4) + to_cols(rel & 15) - 1.0
    g = g_ref[0]
    g_hi = g.astype(MXU_DTYPE)
    r1 = g - g_hi.astype(F32)
    g_mid = r1.astype(MXU_DTYPE)
    g_lo = (r1 - g_mid.astype(F32)).astype(MXU_DTYPE)
    g_col = _dot_nt(eye, g_hi) + _dot_nt(eye, g_mid) + _dot_nt(eye, g_lo)

    lane = lax.broadcasted_iota(I32, (1, kbuf), 1).astype(F32)
    acc = jnp.zeros((tile, x_ref.shape[2]), F32)
    pltpu.make_async_copy(ys_hbm.at[:, 0, pl.ds(0, w_dma), :], buf.at[cur, :, pl.ds(0, w_dma), :],
                          sem.at[cur]).wait()
    for e in range(N_EXPERTS):
        onehot = jnp.where(rel_col[:, e:e + 1] == lane, 1.0, 0.0).astype(MXU_DTYPE)
        acc = acc + g_col[:, e:e + 1] * _dot(onehot, buf[cur, e])
    o_ref[0] = x_ref[0] + mod_ref[0][5:6] * acc


def _combine(a0, xc, mods_i, gates, pos, ys, tile, blk_off, mod_row, w_dma, name):
    bsz, _, d = xc.shape
    n_tok = gates.shape[2]
    n_tiles = n_tok // tile
    kbuf = 2 * LANE

    def mod_map(b, j, a0_ref):
        return (b if mod_row is None else mod_row, 0, 0)

    grid_spec = pltpu.PrefetchScalarGridSpec(
        num_scalar_prefetch=1, grid=(bsz, n_tiles),
        in_specs=[pl.BlockSpec((1, tile, d), lambda b, j, a: (b, blk_off + j, 0)),
                  pl.BlockSpec((1, N_MOD, d), mod_map),
                  pl.BlockSpec((1, N_EXPERTS, tile), lambda b, j, a: (b, 0, j)),
                  pl.BlockSpec((1, N_EXPERTS, tile), lambda b, j, a: (b, 0, j)),
                  pl.BlockSpec(memory_space=pl.ANY)],
        out_specs=pl.BlockSpec((1, tile, d), lambda b, j, a: (b, blk_off + j, 0)),
        scratch_shapes=[pltpu.VMEM((2, N_EXPERTS, kbuf, d), MXU_DTYPE), pltpu.SemaphoreType.DMA((2,))])
    return pl.pallas_call(
        functools.partial(_combine_kernel, w_dma=w_dma, n_tiles=n_tiles),
        out_shape=jax.ShapeDtypeStruct(xc.shape, F32), grid_spec=grid_spec,
        input_output_aliases={1: 0},
        compiler_params=_params(("arbitrary", "arbitrary")), name=name,
    )(a0, xc, mods_i, gates, pos, ys)


def _rope_tables(t_lat, t_ctx, rot_dim, lane_off, period):
    half = rot_dim // 2
    quarter = rot_dim // 4
    t = np.arange(t_lat)
    freqs = ROPE_THETA ** (-np.arange(0, half, 2, dtype=np.float32) / half)
    ang_row = jnp.asarray((t // GRID_W).astype(np.float32))[:, None] * jnp.asarray(freqs)
    ang_col = jnp.asarray((t % GRID_W).astype(np.float32))[:, None] * jnp.asarray(freqs)
    lane = np.arange(LANE)
    u = (lane - lane_off) % period
    active = (lane >= lane_off) & (u < rot_dim)
    is_col = (u // half) == 1
    w = u % half
    fidx = w % quarter
    first = w < quarter
    ang = jnp.where(jnp.asarray(is_col)[None, :], ang_col[:, fidx], ang_row[:, fidx])
    act = jnp.asarray(active)[None, :]
    cos = jnp.where(act, jnp.cos(ang), 1.0)
    sin = jnp.where(act, jnp.sin(ang), 0.0)
    sin_up = jnp.where(jnp.asarray(~first)[None, :], sin, 0.0)
    sin_dn = jnp.where(jnp.asarray(first)[None, :], -sin, 0.0)
    pad = lambda a, v: jnp.concatenate([a, jnp.full((t_ctx, LANE), v, F32)], axis=0)
    return pad(cos.astype(F32), 1.0), pad(sin_up.astype(F32), 0.0), pad(sin_dn.astype(F32), 0.0)


def _lanes(vec, offset=0):
    return jnp.zeros((LANE,), F32).at[offset:offset + vec.shape[0]].set(vec)


def _pad_rows(rows):
    out = jnp.zeros((8, LANE), F32)
    return out.at[:len(rows)].set(jnp.stack(rows))


def _even_weights(w_in, w_out):
    d = w_in.shape[0]
    qb = w_in[:, 1536:2048].reshape(d, 8, HEAD_DIM)
    z = jnp.zeros_like(qb)
    g = (jnp.arange(8) // 4)[None, :, None]
    qb_pad = jnp.concatenate([jnp.where(g == 0, qb, z), jnp.where(g == 1, qb, z)], axis=-1).reshape(d, 8 * LANE)
    w1 = jnp.concatenate([w_in[:, :1536], qb_pad, w_in[:, 2048:]], axis=1).astype(MXU_DTYPE)
    ob = w_out[512:].reshape(8, HEAD_DIM, -1)
    zo = jnp.zeros_like(ob)
    go = (jnp.arange(8) // 4)[:, None, None]
    ob_pad = jnp.concatenate([jnp.where(go == 0, ob, zo), jnp.where(go == 1, ob, zo)], axis=1).reshape(8 * LANE, -1)
    wo = jnp.concatenate([w_out[:512], ob_pad], axis=0).astype(MXU_DTYPE)
    return w1, wo


def _odd_weights(w_in, w_q_up, w_kv_up, w_out):
    d = w_in.shape[0]
    nk = MLA_Q_RANK + MLA_KV_RANK
    w1 = jnp.zeros((d, nk + LANE), F32).at[:, :nk].set(w_in[:, :nk])
    w1 = w1.at[:, nk + MLA_NOPE:nk + MLA_NOPE + MLA_ROPE].set(w_in[:, nk:]).astype(MXU_DTYPE)
    dq = MLA_NOPE + MLA_ROPE
    wq = jnp.pad(w_q_up.reshape(MLA_Q_RANK, MLA_HEADS, dq), ((0, 0), (0, 0), (0, LANE - dq)))
    wq = wq.reshape(MLA_Q_RANK, MLA_HEADS * LANE).astype(MXU_DTYPE)
    kv = w_kv_up.reshape(MLA_KV_RANK, MLA_HEADS, 2 * MLA_NOPE)
    wk = jnp.pad(kv[:, :, :MLA_NOPE], ((0, 0), (0, 0), (0, LANE - MLA_NOPE)))
    wk = wk.reshape(MLA_KV_RANK, MLA_HEADS * LANE).astype(MXU_DTYPE)
    wv = kv[:, :, MLA_NOPE:].reshape(MLA_KV_RANK, MLA_HEADS * MLA_NOPE).astype(MXU_DTYPE)
    ob = w_out.reshape(MLA_HEADS, MLA_NOPE, -1)
    zo = jnp.zeros_like(ob)
    par = (jnp.arange(MLA_HEADS) % 2)[:, None, None]
    wo = jnp.concatenate([jnp.where(par == 0, ob, zo), jnp.where(par == 1, ob, zo)], axis=1)
    wo = wo.reshape(MLA_HEADS * LANE, -1).astype(MXU_DTYPE)
    return w1, wq, wk, wv, wo


def _segments(bounds):
    seg_id = np.zeros((LANE,), np.int32)
    for k, lo in enumerate(bounds):
        seg_id[lo:] = k
    return jnp.asarray(seg_id[:, None] == seg_id[None, :], dtype=MXU_DTYPE)


def _moe_set(xc, h_flat, aff_t, mods_i, weights, layer, *, n_tok, tok_off, tile_c, mod_row, tag):
    bsz, ttot, d = xc.shape
    wg, wu, wd = weights
    cap = EC_CAPACITY_FACTOR * n_tok // N_EXPERTS
    if n_tok % (8 * LANE) == 0:
        gates, pos, idxm = _route_blocked(aff_t, n_tok, tok_off, cap, "route_" + tag)
    else:
        bl = min(2 * LANE, n_tok)
        tri = jnp.asarray(np.triu(np.ones((bl, bl), np.float32)), dtype=MXU_DTYPE)
        gates, pos, idxm = _route(aff_t, tri, n_tok, tok_off // n_tok, cap, "route_" + tag)
    idx = jnp.swapaxes(idxm[:, :, :N_EXPERTS], 1, 2)
    rows = idx + (jnp.arange(bsz, dtype=I32) * ttot + tok_off)[:, None, None]
    tm = min(256, cap)
    rows = jnp.swapaxes(rows, 0, 1).reshape(N_EXPERTS * bsz * (cap // tm), 1, tm)
    ys = _experts(rows, h_flat, wg, wu, wd, layer, bsz, cap, tm, "experts_" + tag)
    w_dma = min(cap, tile_c + 2 * BF16_SUBLANES)
    starts = pos[:, :, ::tile_c]
    a0 = jnp.minimum((starts // BF16_SUBLANES) * BF16_SUBLANES, cap - w_dma)
    a0 = jnp.swapaxes(a0, 1, 2).reshape(-1).astype(I32)
    return _combine(a0, xc, mods_i, gates, pos, ys, tile_c, tok_off // tile_c, mod_row, w_dma, "combine_" + tag)


def kernel(x, c, ctx, c_ctx, w_ada, b_ada, norm_mix, norm_ffn, w_in_even, a_qk_norm, diff_lambda, a_subln,
           b_qk_norm, w_out_even, w_in_odd, mla_q_norm, w_q_up, mla_kv_norm, w_kv_up, mla_qk_norm, w_out_odd,
           w_router, w_exp_gate, w_exp_up, w_exp_down):
    bsz, t_lat, d = x.shape
    t_ctx = ctx.shape[1]
    ttot = t_lat + t_ctx
    assert d == D_MODEL and bsz < MOD_ROWS and t_lat % t_ctx == 0 and t_lat % GRID_W == 0
    tile = min(256, t_ctx)
    tk = min(512, t_lat)
    tq = min(1024, t_lat)
    n_lat_tiles = t_lat // tile

    xc = jnp.concatenate([x, ctx], axis=1)
    c_rows = jnp.zeros((MOD_ROWS, d), F32).at[:bsz].set(c).at[bsz].set(c_ctx)
    mods = _adaln(c_rows, w_ada, b_ada).reshape(DEPTH, MOD_ROWS, N_MOD, d)

    tabs_even = _rope_tables(t_lat, t_ctx, HEAD_DIM, 0, HEAD_DIM)
    tabs_odd = _rope_tables(t_lat, t_ctx, MLA_ROPE, MLA_NOPE, LANE)
    seg_even = _segments([0, HEAD_DIM])
    seg_odd = _segments([0, MLA_NOPE, MLA_NOPE + MLA_ROPE])
    invn_odd = jnp.concatenate([jnp.full((MLA_NOPE,), 1.0 / MLA_NOPE, F32),
                                jnp.full((LANE - MLA_NOPE,), 1.0 / MLA_ROPE, F32)])[None, :]
    expert_w = (w_exp_gate.astype(MXU_DTYPE), w_exp_up.astype(MXU_DTYPE), w_exp_down.astype(MXU_DTYPE))

    for i in range(DEPTH):
        last = i == DEPTH - 1
        j = i // 2
        mods_i = mods[i]
        g_mix = norm_mix[i][None, :]
        if i % 2 == 0:
            w1, wo = _even_weights(w_in_even[j], w_out_even[j])
            gains = _pad_rows([jnp.tile(a_qk_norm[j, 0], 2), jnp.tile(a_qk_norm[j, 1], 2),
                               jnp.tile(b_qk_norm[j, 0], 2), jnp.tile(b_qk_norm[j, 1], 2)])
            q, k, v = _project(_proj_even_kernel, xc, mods_i, [g_mix, w1, seg_even, gains], tabs_even,
                               (16, 5, 5), tile, n_lat_tiles, "proj_even")
            kmap = lambda h: jnp.where(h < 8, h // 2, 4)
            o_lat, o_ctx = _attention(q, k, v, kmap, kmap, t_lat, t_ctx, tq, tk, "attn_even")
            lam_init = 0.8 - 0.6 * math.exp(-0.3 * i)
            extra = [diff_lambda[j], a_subln[j][None, :]]
        else:
            w1, wq, wk, wv, wo = _odd_weights(w_in_odd[j], w_q_up[j], w_kv_up[j], w_out_odd[j])
            qk = mla_qk_norm[j]
            gains = _pad_rows([_lanes(qk[0]), _lanes(qk[1, :MLA_NOPE]), _lanes(qk[1, MLA_NOPE:], MLA_NOPE)])
            consts = [g_mix, w1, mla_q_norm[j][None, :], wq, mla_kv_norm[j][None, :], wk, wv, seg_odd,
                      invn_odd, gains]
            q, k, v = _project(_proj_odd_kernel, xc, mods_i, consts, tabs_odd,
                               (MLA_HEADS, MLA_HEADS, MLA_HEADS // 2), tile, n_lat_tiles, "proj_odd")
            o_lat, o_ctx = _attention(q, k, v, lambda h: h, lambda h: h // 2, t_lat, t_ctx, tq, tk, "attn_odd")
            lam_init = 0.0
            extra = []
        consts = [wo, norm_ffn[i][None, :], jnp.swapaxes(w_router[i], 0, 1)] + extra
        xc, h2, aff_t = _outproj(o_lat, o_ctx, xc, mods_i, consts, i % 2 == 0, lam_init, tile, n_lat_tiles,
                                 "outproj_even" if i % 2 == 0 else "outproj_odd")
        h_flat = h2.reshape(bsz * ttot, d)
        xc = _moe_set(xc, h_flat, aff_t, mods_i, expert_w, i, n_tok=t_lat, tok_off=0,
                      tile_c=min(LANE, t_lat), mod_row=None, tag="lat")
        if not last:
            xc = _moe_set(xc, h_flat, aff_t, mods_i, expert_w, i, n_tok=t_ctx, tok_off=t_lat,
                          tile_c=t_ctx, mod_row=bsz, tag="ctx")
    return xc[:, :t_lat]
```

```python
import functools
import math

import numpy as np
import jax
import jax.numpy as jnp
from jax import lax
from jax.experimental import pallas as pl
from jax.experimental.pallas import tpu as pltpu

F32 = jnp.float32
I32 = jnp.int32
MXU_DTYPE = jnp.bfloat16

D_MODEL = 1024
DEPTH = 4
GRID_W = 64
HEAD_DIM = 64
ROPE_THETA = 10000.0
EPS = 1e-6
N_MOD = 6
N_EXPERTS = 16
EC_CAPACITY_FACTOR = 2
MLA_HEADS = 16
MLA_NOPE = 64
MLA_ROPE = 32
MLA_Q_RANK = 256
MLA_KV_RANK = 128

LANE = 128
BF16_SUBLANES = 16
MOD_ROWS = 16
VMEM_LIMIT = 52 * 1024 * 1024
HIGHEST = lax.Precision.HIGHEST
LOG2_E = math.log2(math.e)
MAX_SOFTMAX_REFERENCE = 56.0
REFERENCE_SLACK = 1.004


def _dot(a, b):
    return jnp.dot(a, b, preferred_element_type=F32)


def _dot_nt(a, b, precision=None):
    return lax.dot_general(a, b, (((1,), (1,)), ((), ())), preferred_element_type=F32, precision=precision)


def _split_bf16(x):
    hi = x.astype(MXU_DTYPE)
    return hi, (x - hi.astype(F32)).astype(MXU_DTYPE)


def _params(sem, vmem=VMEM_LIMIT):
    return pltpu.CompilerParams(dimension_semantics=sem, vmem_limit_bytes=vmem)


def _norm_mod(x, g, shift, scale):
    ms = jnp.mean(x * x, axis=-1, keepdims=True)
    return (x * lax.rsqrt(ms + EPS)) * g * (1.0 + scale) + shift


def _mod_kernel(c_ref, w_ref, b_ref, o_ref):
    c = c_ref[...]
    sc = c * jax.nn.sigmoid(c)
    o_ref[0] = jnp.dot(sc, w_ref[0], preferred_element_type=F32, precision=HIGHEST) + b_ref[0]


def _adaln(c_rows, w_ada, b_ada):
    depth, d, n = w_ada.shape
    nb = n // 4
    return pl.pallas_call(
        _mod_kernel,
        out_shape=jax.ShapeDtypeStruct((depth, MOD_ROWS, n), F32),
        grid=(depth, n // nb),
        in_specs=[pl.BlockSpec((MOD_ROWS, d), lambda i, j: (0, 0)),
                  pl.BlockSpec((1, d, nb), lambda i, j: (i, 0, j)),
                  pl.BlockSpec((1, 1, nb), lambda i, j: (i, 0, j))],
        out_specs=pl.BlockSpec((1, MOD_ROWS, nb), lambda i, j: (i, 0, j)),
        compiler_params=_params(("arbitrary", "arbitrary")),
        name="adaln",
    )(c_rows, w_ada, b_ada.reshape(depth, 1, n))


def _seg_normrope(blk, seg, inv_n, gain, tabs, dist):
    ss = _dot((blk * blk).astype(MXU_DTYPE), seg)
    xn = blk * lax.rsqrt(ss * inv_n + EPS) * gain
    if tabs is not None:
        cos, sin_up, sin_dn = tabs
        xn = xn * cos + pltpu.roll(xn, dist, 1) * sin_up + pltpu.roll(xn, LANE - dist, 1) * sin_dn
    return xn


def _proj_even_kernel(x_ref, mod_ref, g_ref, w_ref, seg_ref, gains_ref, cos_ref, sup_ref, sdn_ref,
                      q_ref, k_ref, v_ref, vt_ref):
    mod = mod_ref[0]
    h = _norm_mod(x_ref[0], g_ref[...], mod[0:1], mod[1:2]).astype(MXU_DTYPE)
    y = _dot(h, w_ref[...])
    seg = seg_ref[...]
    tabs = (cos_ref[...], sup_ref[...], sdn_ref[...])
    inv_n = 1.0 / HEAD_DIM
    scale = HEAD_DIM ** -0.5 * LOG2_E
    lane = lax.broadcasted_iota(I32, (1, LANE), 1)
    first = (lane < HEAD_DIM).astype(F32)
    second = 1.0 - first
    dist = HEAD_DIM // 4

    def blk(j):
        return y[:, LANE * j:LANE * (j + 1)]

    for hh in range(4):
        qa = _seg_normrope(blk(hh), seg, inv_n, gains_ref[0:1], tabs, dist) * scale
        q_ref[0, 2 * hh] = (qa * first).astype(q_ref.dtype)
        q_ref[0, 2 * hh + 1] = (qa * second).astype(q_ref.dtype)
        ka = _seg_normrope(blk(4 + hh), seg, inv_n, gains_ref[1:2], tabs, dist)
        k_ref[0, hh] = ka.astype(k_ref.dtype)
        v_ref[0, hh] = blk(8 + hh).astype(v_ref.dtype)
        vt_ref[0, hh] = blk(8 + hh).T.astype(vt_ref.dtype)
    for hh in range(8):
        qb = _seg_normrope(blk(12 + hh), seg, inv_n, gains_ref[2:3], tabs, dist) * scale
        q_ref[0, 8 + hh] = qb.astype(q_ref.dtype)
    kb = _seg_normrope(blk(20), seg, inv_n, gains_ref[3:4], tabs, dist)
    k_ref[0, 4] = kb.astype(k_ref.dtype)
    v_ref[0, 4] = blk(21).astype(v_ref.dtype)
    vt_ref[0, 4] = blk(21).T.astype(vt_ref.dtype)


def _proj_odd_kernel(x_ref, mod_ref, g_ref, w1_ref, qn_ref, wq_ref, kvn_ref, wk_ref, wv_ref, seg_ref,
                     invn_ref, gains_ref, cos_ref, sup_ref, sdn_ref, q_ref, k_ref, v_ref, vt_ref):
    mod = mod_ref[0]
    h = _norm_mod(x_ref[0], g_ref[...], mod[0:1], mod[1:2]).astype(MXU_DTYPE)
    y1 = _dot(h, w1_ref[...])
    cq = y1[:, :MLA_Q_RANK]
    cqn = (cq * lax.rsqrt(jnp.mean(cq * cq, axis=-1, keepdims=True) + EPS) * qn_ref[...]).astype(MXU_DTYPE)
    ckv = y1[:, MLA_Q_RANK:MLA_Q_RANK + MLA_KV_RANK]
    ckvn = (ckv * lax.rsqrt(jnp.mean(ckv * ckv, axis=-1, keepdims=True) + EPS) * kvn_ref[...]).astype(MXU_DTYPE)
    qf = _dot(cqn, wq_ref[...])
    kf = _dot(ckvn, wk_ref[...])
    vf = _dot(ckvn, wv_ref[...])
    seg = seg_ref[...]
    inv_n = invn_ref[...]
    tabs = (cos_ref[...], sup_ref[...], sdn_ref[...])
    scale = (MLA_NOPE + MLA_ROPE) ** -0.5 * LOG2_E
    dist = MLA_ROPE // 4
    kr = _seg_normrope(y1[:, MLA_Q_RANK + MLA_KV_RANK:], seg, inv_n, gains_ref[2:3], tabs, dist)
    for hh in range(MLA_HEADS):
        qh = _seg_normrope(qf[:, LANE * hh:LANE * (hh + 1)], seg, inv_n, gains_ref[0:1], tabs, dist) * scale
        q_ref[0, hh] = qh.astype(q_ref.dtype)
        kh = _seg_normrope(kf[:, LANE * hh:LANE * (hh + 1)], seg, inv_n, gains_ref[1:2], None, 0) + kr
        k_ref[0, hh] = kh.astype(k_ref.dtype)
    for j in range(MLA_HEADS // 2):
        v_ref[0, j] = vf[:, LANE * j:LANE * (j + 1)].astype(v_ref.dtype)
        vt_ref[0, j] = vf[:, LANE * j:LANE * (j + 1)].T.astype(vt_ref.dtype)


def _full(shape):
    zeros = (0,) * len(shape)
    return pl.BlockSpec(shape, lambda b, j: zeros)


def _project(kernel_fn, xc, mods_i, consts, tabs, heads, tile, n_lat_tiles, name):
    bsz, ttot, d = xc.shape
    n_tiles = ttot // tile
    nq, nk, nv = heads

    def mod_map(b, j):
        return (jnp.where(j < n_lat_tiles, b, bsz), 0, 0)

    in_specs = [pl.BlockSpec((1, tile, d), lambda b, j: (b, j, 0)),
                pl.BlockSpec((1, N_MOD, d), mod_map)]
    in_specs += [_full(a.shape) for a in consts]
    in_specs += [pl.BlockSpec((tile, LANE), lambda b, j: (j, 0))] * 3
    out_shape = [jax.ShapeDtypeStruct((bsz, n, ttot, LANE), MXU_DTYPE) for n in (nq, nk, nv)]
    out_specs = [pl.BlockSpec((1, n, tile, LANE), lambda b, j: (b, 0, j, 0)) for n in (nq, nk, nv)]
    out_shape.append(jax.ShapeDtypeStruct((bsz, nv, LANE, ttot), MXU_DTYPE))
    out_specs.append(pl.BlockSpec((1, nv, LANE, tile), lambda b, j: (b, 0, 0, j)))
    return pl.pallas_call(
        kernel_fn, out_shape=out_shape, grid=(bsz, n_tiles), in_specs=in_specs, out_specs=out_specs,
        compiler_params=_params(("arbitrary", "arbitrary")), name=name,
    )(xc, mods_i, *consts, *tabs)


def _attn_kernel(q_ref, k_ref, v_ref, vt_ref, o_ref, knorm_scr, *, chunks):
    q = q_ref[0, 0]
    tq = q.shape[0]

    def kv(lo, size):
        return k_ref[0, 0, lo:lo + size, :], v_ref[0, 0, lo:lo + size, :]

    @pl.when(pl.program_id(2) == 0)
    def _():
        kmax2 = jnp.zeros((1, 1), F32)
        for lo, size in chunks:
            kf = kv(lo, size)[0].astype(F32)
            r = jnp.sum(kf * kf, axis=-1, keepdims=True)
            kmax2 = jnp.maximum(kmax2, jnp.max(r, axis=0, keepdims=True))
        knorm_scr[...] = jnp.broadcast_to(jnp.sqrt(kmax2), knorm_scr.shape)

    qf = q.astype(F32)
    ref_row = jnp.sqrt(jnp.sum(qf * qf, axis=-1, keepdims=True)) * knorm_scr[0:1, 0:1]

    def fixed_reference(_):
        ones = jnp.ones((8, LANE), MXU_DTYPE)
        ref_t = jnp.sqrt(_dot_nt(ones, (qf * qf).astype(MXU_DTYPE))[0:1]) * (REFERENCE_SLACK * knorm_scr[0:1, 0:1])
        l = jnp.zeros((8, tq), F32)
        acc = jnp.zeros((LANE, tq), F32)
        for lo, size in chunks:
            k = k_ref[0, 0, lo:lo + size, :]
            vt = vt_ref[0, 0, :, lo:lo + size]
            p = jnp.exp2(_dot_nt(k, q) - ref_t)
            l = l + jnp.sum(p.reshape(size // 8, 8, tq), axis=0)
            acc = acc + _dot(vt, p.astype(vt.dtype))
        out_t = acc / jnp.sum(l, axis=0, keepdims=True)
        return out_t.T

    def running_max(_):
        def step(k, v, carry):
            m, l, acc = carry
            s = _dot_nt(q, k)
            m_new = jnp.maximum(m, jnp.max(s, axis=-1, keepdims=True))
            alpha = jnp.exp2(m - m_new)
            p = jnp.exp2(s - m_new)
            l = alpha * l + jnp.sum(p, axis=-1, keepdims=True)
            return m_new, l, alpha * acc + _dot(p.astype(v.dtype), v)

        carry = (jnp.full((tq, 1), -1e30, F32), jnp.zeros((tq, 1), F32), jnp.zeros((tq, LANE), F32))
        size0 = chunks[0][1]
        n_uniform = sum(1 for lo, size in chunks if size == size0 and lo % size0 == 0)

        def body(c, cr):
            lo = pl.multiple_of(c * size0, size0)
            return step(k_ref[0, 0, pl.ds(lo, size0), :], v_ref[0, 0, pl.ds(lo, size0), :], cr)

        carry = lax.fori_loop(0, n_uniform, body, carry)
        for lo, size in chunks[n_uniform:]:
            carry = step(*kv(lo, size), carry)
        _, l, acc = carry
        return acc / l

    in_range = jnp.max(ref_row) < MAX_SOFTMAX_REFERENCE
    o_ref[0, 0] = lax.cond(in_range, fixed_reference, running_max, 0)


def _attention(q, k, v, vt, kmap, vmap, t_lat, t_ctx, tq, tk, name):
    bsz, nq, ttot, _ = q.shape
    chunks = [(c * tk, tk) for c in range(t_lat // tk)] + [(t_lat, t_ctx)]
    sem = ("arbitrary", "arbitrary", "arbitrary")
    scratch = [pltpu.VMEM((8, LANE), F32)]
    o_lat = pl.pallas_call(
        functools.partial(_attn_kernel, chunks=chunks),
        out_shape=jax.ShapeDtypeStruct((bsz, nq, t_lat, LANE), F32), grid=(bsz, nq, t_lat // tq),
        in_specs=[pl.BlockSpec((1, 1, tq, LANE), lambda b, h, t: (b, h, t, 0)),
                  pl.BlockSpec((1, 1, ttot, LANE), lambda b, h, t: (b, kmap(h), 0, 0)),
                  pl.BlockSpec((1, 1, ttot, LANE), lambda b, h, t: (b, vmap(h), 0, 0)),
                  pl.BlockSpec((1, 1, LANE, ttot), lambda b, h, t: (b, vmap(h), 0, 0))],
        out_specs=pl.BlockSpec((1, 1, tq, LANE), lambda b, h, t: (b, h, t, 0)),
        scratch_shapes=scratch, compiler_params=_params(sem), name=name,
    )(q, k, v, vt)
    ctx_blk = t_lat // t_ctx
    o_ctx = pl.pallas_call(
        functools.partial(_attn_kernel, chunks=[(0, t_ctx)]),
        out_shape=jax.ShapeDtypeStruct((bsz, nq, t_ctx, LANE), F32), grid=(bsz, nq, 1),
        in_specs=[pl.BlockSpec((1, 1, t_ctx, LANE), lambda b, h, t: (b, h, ctx_blk, 0)),
                  pl.BlockSpec((1, 1, t_ctx, LANE), lambda b, h, t: (b, kmap(h), ctx_blk, 0)),
                  pl.BlockSpec((1, 1, t_ctx, LANE), lambda b, h, t: (b, vmap(h), ctx_blk, 0)),
                  pl.BlockSpec((1, 1, LANE, t_ctx), lambda b, h, t: (b, vmap(h), 0, ctx_blk))],
        out_specs=pl.BlockSpec((1, 1, t_ctx, LANE), lambda b, h, t: (b, h, 0, 0)),
        scratch_shapes=scratch, compiler_params=_params(sem), name=name + "_ctx",
    )(q, k, v, vt)
    return o_lat, o_ctx


def _outproj_kernel(*refs, even, lam_init, n_lat_tiles):
    if even:
        (ol_ref, oc_ref, x_ref, mod_ref, wout_ref, gffn_ref, wr_ref, lam_ref, subln_ref,
         xo_ref, h2_ref, aff_ref) = refs
    else:
        ol_ref, oc_ref, x_ref, mod_ref, wout_ref, gffn_ref, wr_ref, xo_ref, h2_ref, aff_ref = refs
    mod = mod_ref[0]
    is_ctx = pl.program_id(1) >= n_lat_tiles

    def head(hh):
        return jnp.where(is_ctx, oc_ref[0, hh], ol_ref[0, hh])

    if even:
        lv = lam_ref[...]
        lam = (jnp.exp(jnp.sum(lv[0:1] * lv[1:2], axis=-1, keepdims=True))
               - jnp.exp(jnp.sum(lv[2:3] * lv[3:4], axis=-1, keepdims=True)) + lam_init)
        parts = []
        for hh in range(4):
            oa = head(2 * hh) - lam * head(2 * hh + 1)
            oa = oa * lax.rsqrt(jnp.mean(oa * oa, axis=-1, keepdims=True) + EPS) * subln_ref[...]
            parts.append((oa * (1.0 - lam_init)).astype(MXU_DTYPE))
        for hh in range(8):
            parts.append(head(8 + hh).astype(MXU_DTYPE))
    else:
        parts = [head(hh).astype(MXU_DTYPE) for hh in range(MLA_HEADS)]
    y = _dot(jnp.concatenate(parts, axis=1), wout_ref[...])
    xn = x_ref[0] + mod[2:3] * y
    xo_ref[0] = xn
    h2 = _norm_mod(xn, gffn_ref[...], mod[3:4], mod[4:5])
    h2_ref[0] = h2
    w_hi, w_lo = _split_bf16(wr_ref[...])
    h_hi, h_lo = _split_bf16(h2)
    logits = _dot_nt(w_hi, h_hi) + (_dot_nt(w_hi, h_lo) + _dot_nt(w_lo, h_hi))
    e = jnp.exp(logits - jnp.max(logits, axis=0, keepdims=True))
    aff_ref[0] = e / jnp.sum(e, axis=0, keepdims=True)


def _outproj(o_lat, o_ctx, xc, mods_i, consts, even, lam_init, tile, n_lat_tiles, name):
    bsz, ttot, d = xc.shape
    nq = o_lat.shape[1]

    def mod_map(b, j):
        return (jnp.where(j < n_lat_tiles, b, bsz), 0, 0)

    in_specs = [pl.BlockSpec((1, nq, tile, LANE), lambda b, j: (b, 0, jnp.minimum(j, n_lat_tiles - 1), 0)),
                pl.BlockSpec((1, nq, tile, LANE), lambda b, j: (b, 0, jnp.maximum(j - n_lat_tiles, 0), 0)),
                pl.BlockSpec((1, tile, d), lambda b, j: (b, j, 0)),
                pl.BlockSpec((1, N_MOD, d), mod_map)]
    in_specs += [_full(a.shape) for a in consts]
    out_shape = [jax.ShapeDtypeStruct((bsz, ttot, d), F32), jax.ShapeDtypeStruct((bsz, ttot, d), F32),
                 jax.ShapeDtypeStruct((bsz, N_EXPERTS, ttot), F32)]
    out_specs = [pl.BlockSpec((1, tile, d), lambda b, j: (b, j, 0)),
                 pl.BlockSpec((1, tile, d), lambda b, j: (b, j, 0)),
                 pl.BlockSpec((1, N_EXPERTS, tile), lambda b, j: (b, 0, j))]
    return pl.pallas_call(
        functools.partial(_outproj_kernel, even=even, lam_init=lam_init, n_lat_tiles=n_lat_tiles),
        out_shape=out_shape, grid=(bsz, ttot // tile), in_specs=in_specs, out_specs=out_specs,
        compiler_params=_params(("arbitrary", "arbitrary")), name=name,
    )(o_lat, o_ctx, xc, mods_i, *consts)


def _cumsum_lanes(x, tri):
    bl = tri.shape[0]
    outs = []
    carry = jnp.zeros((x.shape[0], 1), F32)
    for c in range(x.shape[1] // bl):
        r = _dot(x[:, c * bl:(c + 1) * bl].astype(MXU_DTYPE), tri) + carry
        outs.append(r)
        carry = r[:, bl - 1:bl]
    return outs[0] if len(outs) == 1 else jnp.concatenate(outs, axis=1)


def _route_kernel(aff_ref, tri_ref, g_ref, pos_ref, idx_ref, cum_scr, *, cap, chunk):
    a = aff_ref[0]
    n_tok = a.shape[1]
    bits = pltpu.bitcast(a, I32)

    thr = _threshold_bits(bits, cap)
    gt = bits > thr
    eq = bits == thr
    need = cap - jnp.sum(gt.astype(F32), axis=1, keepdims=True)
    eqf = eq.astype(F32)
    tri = tri_ref[...]
    eq_rank = _cumsum_lanes(eqf, tri) - eqf
    sel = jnp.logical_or(gt, jnp.logical_and(eq, eq_rank < need))
    self = sel.astype(F32)
    cum = _cumsum_lanes(self, tri)
    g_ref[0] = jnp.where(sel, a, 0.0)
    pos_ref[0] = (cum - self).astype(I32)

    n_chunks = n_tok // chunk
    for c in range(n_chunks):
        cum_scr[c] = cum[:, c * chunk:(c + 1) * chunk]
    slot = lax.broadcasted_iota(I32, (cap, 1), 0).astype(F32)
    lane = lax.broadcasted_iota(I32, (1, LANE), 1)
    idxmat = jnp.zeros((cap, LANE), F32)
    for e in range(N_EXPERTS):
        def count(c, acc):
            return acc + jnp.where(cum_scr[c, e:e + 1, :] <= slot, 1.0, 0.0)

        acc = lax.fori_loop(0, n_chunks, count, jnp.zeros((cap, chunk), F32))
        idxmat = idxmat + jnp.sum(acc, axis=1, keepdims=True) * (lane == e).astype(F32)
    idx_ref[0] = idxmat.astype(I32)


def _threshold_bits(bits, cap):
    def search(i, cur):
        cand = cur | lax.shift_left(jnp.int32(1), 30 - i)
        cnt = jnp.sum((bits >= cand).astype(I32), axis=1, keepdims=True)
        return jnp.where(cnt >= cap, cand, cur)

    return lax.fori_loop(0, 31, search, jnp.zeros((bits.shape[0], 1), I32))


def _route_blocked_kernel(aff_ref, aff4_ref, tri_ref, low_ref, upp_ref, g_ref, pos_ref, idx_ref, *, cap):
    bits2 = pltpu.bitcast(aff_ref[0], I32)
    thr = _threshold_bits(bits2, cap)
    need = cap - jnp.sum((bits2 > thr).astype(F32), axis=1, keepdims=True)
    tri, low, upp = tri_ref[...], low_ref[...], upp_ref[...]
    nb = aff4_ref.shape[2]
    ones8 = jnp.ones((8, LANE), MXU_DTYPE)
    slot = lax.broadcasted_iota(I32, (cap, 1), 0).astype(F32)
    row_id = lax.broadcasted_iota(I32, (1, nb), 1).astype(F32)
    lane = lax.broadcasted_iota(I32, (1, LANE), 1)

    def prefix(x01):
        within = _dot(x01.astype(MXU_DTYPE), tri)
        tot = jnp.broadcast_to(within[:, LANE - 1:LANE], (nb, LANE)).astype(MXU_DTYPE)
        return within, _dot(low, tot)

    idxmat = jnp.zeros((cap, LANE), F32)
    for e in range(N_EXPERTS):
        a = aff4_ref[0, e]
        b = pltpu.bitcast(a, I32)
        t = thr[e:e + 1, :]
        gt = b > t
        eq = b == t
        eqf = eq.astype(F32)
        w_eq, off_eq = prefix(eqf)
        tie_ok = jnp.logical_and(eq, (w_eq + off_eq - eqf) < need[e:e + 1, :])
        sel = jnp.logical_or(gt, tie_ok)
        self = sel.astype(F32)
        within, rowoff = prefix(self)
        g_ref[0, e] = jnp.where(sel, a, 0.0)
        pos_ref[0, e] = (within + rowoff - self).astype(I32)
        tot_l = _dot_nt(ones8, self.astype(MXU_DTYPE))
        start_l = _dot(tot_l.astype(MXU_DTYPE), upp)[0:1]
        end_l = start_l + tot_l[0:1]
        inrow = jnp.where(start_l <= slot, jnp.where(slot < end_l, 1.0, 0.0), 0.0)
        local = _dot(inrow.astype(MXU_DTYPE), within.astype(MXU_DTYPE))
        rank = slot - jnp.sum(inrow * start_l, axis=1, keepdims=True)
        col = jnp.sum(jnp.where(local <= rank, 1.0, 0.0), axis=1, keepdims=True)
        row = jnp.sum(inrow * row_id, axis=1, keepdims=True)
        idxmat = idxmat + (row * LANE + col) * (lane == e).astype(F32)
    idx_ref[0] = idxmat.astype(I32)


def _route_blocked(aff_t, n_tok, tok_off, cap, name):
    bsz = aff_t.shape[0]
    nb = n_tok // LANE
    aff4 = aff_t[:, :, tok_off:tok_off + n_tok].reshape(bsz, N_EXPERTS, nb, LANE)
    tri = jnp.asarray(np.triu(np.ones((LANE, LANE), np.float32)), dtype=MXU_DTYPE)
    low = jnp.asarray(np.tril(np.ones((nb, nb), np.float32), -1), dtype=MXU_DTYPE)
    upp = jnp.asarray(np.triu(np.ones((nb, nb), np.float32), 1), dtype=MXU_DTYPE)
    blk4 = pl.BlockSpec((1, N_EXPERTS, nb, LANE), lambda b: (b, 0, 0, 0))
    g4, pos4, idxm = pl.pallas_call(
        functools.partial(_route_blocked_kernel, cap=cap),
        out_shape=[jax.ShapeDtypeStruct((bsz, N_EXPERTS, nb, LANE), F32),
                   jax.ShapeDtypeStruct((bsz, N_EXPERTS, nb, LANE), I32),
                   jax.ShapeDtypeStruct((bsz, cap, LANE), I32)],
        grid=(bsz,),
        in_specs=[pl.BlockSpec((1, N_EXPERTS, n_tok), lambda b: (b, 0, tok_off // n_tok)), blk4,
                  pl.BlockSpec(tri.shape, lambda b: (0, 0)), pl.BlockSpec(low.shape, lambda b: (0, 0)),
                  pl.BlockSpec(upp.shape, lambda b: (0, 0))],
        out_specs=[blk4, blk4, pl.BlockSpec((1, cap, LANE), lambda b: (b, 0, 0))],
        compiler_params=_params(("arbitrary",)), name=name,
    )(aff_t, aff4, tri, low, upp)
    return g4.reshape(bsz, N_EXPERTS, n_tok), pos4.reshape(bsz, N_EXPERTS, n_tok), idxm


def _route(aff_t, tri, n_tok, blk_off, cap, name):
    bsz = aff_t.shape[0]
    chunk = min(512, n_tok)
    kern = functools.partial(_route_kernel, cap=cap, chunk=chunk)
    return pl.pallas_call(
        kern,
        out_shape=[jax.ShapeDtypeStruct((bsz, N_EXPERTS, n_tok), F32),
                   jax.ShapeDtypeStruct((bsz, N_EXPERTS, n_tok), I32),
                   jax.ShapeDtypeStruct((bsz, cap, LANE), I32)],
        grid=(bsz,),
        in_specs=[pl.BlockSpec((1, N_EXPERTS, n_tok), lambda b: (b, 0, blk_off)),
                  pl.BlockSpec(tri.shape, lambda b: (0, 0))],
        out_specs=[pl.BlockSpec((1, N_EXPERTS, n_tok), lambda b: (b, 0, 0)),
                   pl.BlockSpec((1, N_EXPERTS, n_tok), lambda b: (b, 0, 0)),
                   pl.BlockSpec((1, cap, LANE), lambda b: (b, 0, 0))],
        scratch_shapes=[pltpu.VMEM((n_tok // chunk, N_EXPERTS, chunk), F32)],
        compiler_params=_params(("arbitrary",)), name=name,
    )(aff_t, tri)


def _expert_kernel(idx_ref, nxt_ref, h_hbm, wg_ref, wu_ref, wd_ref, y_ref, buf, sem):
    tm = buf.shape[1]
    step = (pl.program_id(0) * pl.num_programs(1) + pl.program_id(1)) * pl.num_programs(2) + pl.program_id(2)
    n_steps = pl.num_programs(0) * pl.num_programs(1) * pl.num_programs(2)
    cur = step % 2

    def gather(rows_ref, half):
        for r in range(tm):
            pltpu.make_async_copy(h_hbm.at[pl.ds(rows_ref[0, 0, r], 1), :], buf.at[half, pl.ds(r, 1), :],
                                  sem.at[half]).start()

    def wait(half):
        pltpu.make_async_copy(h_hbm.at[pl.ds(0, tm), :], buf.at[half], sem.at[half]).wait()

    @pl.when(step == 0)
    def _():
        gather(idx_ref, cur)

    gather(nxt_ref, 1 - cur)
    wait(cur)
    x = buf[cur].astype(MXU_DTYPE)
    a = _dot(x, wg_ref[0, 0])
    u = _dot(x, wu_ref[0, 0])
    hmid = (a * jax.nn.sigmoid(a) * u).astype(MXU_DTYPE)
    y_ref[0, 0] = _dot(hmid, wd_ref[0, 0]).astype(y_ref.dtype)

    @pl.when(step == n_steps - 1)
    def _():
        wait(1 - cur)


def _experts(rows, h_flat, wg, wu, wd, layer, bsz, cap, tm, name):
    d = h_flat.shape[1]
    nt = cap // tm
    last = N_EXPERTS * bsz * nt - 1
    wspec = pl.BlockSpec((1, 1, d, d), lambda e, b, t: (layer, e, 0, 0))

    def tile_id(e, b, t):
        return (e * bsz + b) * nt + t

    return pl.pallas_call(
        _expert_kernel,
        out_shape=jax.ShapeDtypeStruct((N_EXPERTS, bsz, cap, d), MXU_DTYPE),
        grid=(N_EXPERTS, bsz, nt),
        in_specs=[pl.BlockSpec((1, 1, tm), lambda e, b, t: (tile_id(e, b, t), 0, 0), memory_space=pltpu.SMEM),
                  pl.BlockSpec((1, 1, tm), lambda e, b, t: (jnp.minimum(tile_id(e, b, t) + 1, last), 0, 0),
                               memory_space=pltpu.SMEM),
                  pl.BlockSpec(memory_space=pl.ANY), wspec, wspec, wspec],
        out_specs=pl.BlockSpec((1, 1, tm, d), lambda e, b, t: (e, b, t, 0)),
        scratch_shapes=[pltpu.VMEM((2, tm, d), F32), pltpu.SemaphoreType.DMA((2,))],
        compiler_params=_params(("arbitrary", "arbitrary", "arbitrary")), name=name,
    )(rows, rows, h_flat, wg, wu, wd)


def _combine_kernel(a0_ref, x_ref, mod_ref, g_ref, pos_ref, ys_hbm, o_ref, buf, sem, *, w_dma, n_tiles):
    tile = x_ref.shape[1]
    kbuf = buf.shape[2]
    step = pl.program_id(0) * n_tiles + pl.program_id(1)
    n_steps = pl.num_programs(0) * n_tiles
    cur = step % 2
    base = step * N_EXPERTS

    def window(e, at_step, half):
        a0 = pl.multiple_of(a0_ref[at_step * N_EXPERTS + e], BF16_SUBLANES)
        return pltpu.make_async_copy(ys_hbm.at[e, at_step // n_tiles, pl.ds(a0, w_dma), :],
                                     buf.at[half, e, pl.ds(0, w_dma), :], sem.at[half])

    @pl.when(step == 0)
    def _():
        if w_dma < kbuf:
            buf[:, :, w_dma:, :] = jnp.zeros((2, N_EXPERTS, kbuf - w_dma, buf.shape[3]), buf.dtype)
        for e in range(N_EXPERTS):
            window(e, step, cur).start()

    @pl.when(step + 1 < n_steps)
    def _():
        for e in range(N_EXPERTS):
            window(e, step + 1, 1 - cur).start()

    sub = lax.broadcasted_iota(I32, (N_EXPERTS, 1), 0)
    a0v = jnp.zeros((N_EXPERTS, 1), I32)
    for e in range(N_EXPERTS):
        a0v = jnp.where(sub == e, a0_ref[base + e], a0v)
    rel = jnp.clip(pos_ref[0] - a0v, -1, kbuf) + 1
    eye = (lax.broadcasted_iota(I32, (tile, tile), 0) == lax.broadcasted_iota(I32, (tile, tile), 1)
           ).astype(MXU_DTYPE)

    def to_cols(m):
        return _dot_nt(eye, m.astype(MXU_DTYPE))

    rel_col = 16.0 * to_cols(rel >> 4) + to_cols(rel & 15) - 1.0
    g = g_ref[0]
    g_hi = g.astype(MXU_DTYPE)
    r1 = g - g_hi.astype(F32)
    g_mid = r1.astype(MXU_DTYPE)
    g_lo = (r1 - g_mid.astype(F32)).astype(MXU_DTYPE)
    g_col = _dot_nt(eye, g_hi) + _dot_nt(eye, g_mid) + _dot_nt(eye, g_lo)

    lane = lax.broadcasted_iota(I32, (1, kbuf), 1).astype(F32)
    acc = jnp.zeros((tile, x_ref.shape[2]), F32)
    pltpu.make_async_copy(ys_hbm.at[:, 0, pl.ds(0, w_dma), :], buf.at[cur, :, pl.ds(0, w_dma), :],
                          sem.at[cur]).wait()
    for e in range(N_EXPERTS):
        onehot = jnp.where(rel_col[:, e:e + 1] == lane, 1.0, 0.0).astype(MXU_DTYPE)
        acc = acc + g_col[:, e:e + 1] * _dot(onehot, buf[cur, e])
    o_ref[0] = x_ref[0] + mod_ref[0][5:6] * acc


def _combine(a0, xc, mods_i, gates, pos, ys, tile, blk_off, mod_row, w_dma, name):
    bsz, _, d = xc.shape
    n_tok = gates.shape[2]
    n_tiles = n_tok // tile
    kbuf = 2 * LANE

    def mod_map(b, j, a0_ref):
        return (b if mod_row is None else mod_row, 0, 0)

    grid_spec = pltpu.PrefetchScalarGridSpec(
        num_scalar_prefetch=1, grid=(bsz, n_tiles),
        in_specs=[pl.BlockSpec((1, tile, d), lambda b, j, a: (b, blk_off + j, 0)),
                  pl.BlockSpec((1, N_MOD, d), mod_map),
                  pl.BlockSpec((1, N_EXPERTS, tile), lambda b, j, a: (b, 0, j)),
                  pl.BlockSpec((1, N_EXPERTS, tile), lambda b, j, a: (b, 0, j)),
                  pl.BlockSpec(memory_space=pl.ANY)],
        out_specs=pl.BlockSpec((1, tile, d), lambda b, j, a: (b, blk_off + j, 0)),
        scratch_shapes=[pltpu.VMEM((2, N_EXPERTS, kbuf, d), MXU_DTYPE), pltpu.SemaphoreType.DMA((2,))])
    return pl.pallas_call(
        functools.partial(_combine_kernel, w_dma=w_dma, n_tiles=n_tiles),
        out_shape=jax.ShapeDtypeStruct(xc.shape, F32), grid_spec=grid_spec,
        input_output_aliases={1: 0},
        compiler_params=_params(("arbitrary", "arbitrary")), name=name,
    )(a0, xc, mods_i, gates, pos, ys)


def _rope_tables(t_lat, t_ctx, rot_dim, lane_off, period):
    half = rot_dim // 2
    quarter = rot_dim // 4
    t = np.arange(t_lat)
    freqs = ROPE_THETA ** (-np.arange(0, half, 2, dtype=np.float32) / half)
    ang_row = jnp.asarray((t // GRID_W).astype(np.float32))[:, None] * jnp.asarray(freqs)
    ang_col = jnp.asarray((t % GRID_W).astype(np.float32))[:, None] * jnp.asarray(freqs)
    lane = np.arange(LANE)
    u = (lane - lane_off) % period
    active = (lane >= lane_off) & (u < rot_dim)
    is_col = (u // half) == 1
    w = u % half
    fidx = w % quarter
    first = w < quarter
    ang = jnp.where(jnp.asarray(is_col)[None, :], ang_col[:, fidx], ang_row[:, fidx])
    act = jnp.asarray(active)[None, :]
    cos = jnp.where(act, jnp.cos(ang), 1.0)
    sin = jnp.where(act, jnp.sin(ang), 0.0)
    sin_up = jnp.where(jnp.asarray(~first)[None, :], sin, 0.0)
    sin_dn = jnp.where(jnp.asarray(first)[None, :], -sin, 0.0)
    pad = lambda a, v: jnp.concatenate([a, jnp.full((t_ctx, LANE), v, F32)], axis=0)
    return pad(cos.astype(F32), 1.0), pad(sin_up.astype(F32), 0.0), pad(sin_dn.astype(F32), 0.0)


def _lanes(vec, offset=0):
    return jnp.zeros((LANE,), F32).at[offset:offset + vec.shape[0]].set(vec)


def _pad_rows(rows):
    out = jnp.zeros((8, LANE), F32)
    return out.at[:len(rows)].set(jnp.stack(rows))


def _even_weights(w_in, w_out):
    d = w_in.shape[0]
    qb = w_in[:, 1536:2048].reshape(d, 8, HEAD_DIM)
    z = jnp.zeros_like(qb)
    g = (jnp.arange(8) // 4)[None, :, None]
    qb_pad = jnp.concatenate([jnp.where(g == 0, qb, z), jnp.where(g == 1, qb, z)], axis=-1).reshape(d, 8 * LANE)
    w1 = jnp.concatenate([w_in[:, :1536], qb_pad, w_in[:, 2048:]], axis=1).astype(MXU_DTYPE)
    ob = w_out[512:].reshape(8, HEAD_DIM, -1)
    zo = jnp.zeros_like(ob)
    go = (jnp.arange(8) // 4)[:, None, None]
    ob_pad = jnp.concatenate([jnp.where(go == 0, ob, zo), jnp.where(go == 1, ob, zo)], axis=1).reshape(8 * LANE, -1)
    wo = jnp.concatenate([w_out[:512], ob_pad], axis=0).astype(MXU_DTYPE)
    return w1, wo


def _odd_weights(w_in, w_q_up, w_kv_up, w_out):
    d = w_in.shape[0]
    nk = MLA_Q_RANK + MLA_KV_RANK
    w1 = jnp.zeros((d, nk + LANE), F32).at[:, :nk].set(w_in[:, :nk])
    w1 = w1.at[:, nk + MLA_NOPE:nk + MLA_NOPE + MLA_ROPE].set(w_in[:, nk:]).astype(MXU_DTYPE)
    dq = MLA_NOPE + MLA_ROPE
    wq = jnp.pad(w_q_up.reshape(MLA_Q_RANK, MLA_HEADS, dq), ((0, 0), (0, 0), (0, LANE - dq)))
    wq = wq.reshape(MLA_Q_RANK, MLA_HEADS * LANE).astype(MXU_DTYPE)
    kv = w_kv_up.reshape(MLA_KV_RANK, MLA_HEADS, 2 * MLA_NOPE)
    wk = jnp.pad(kv[:, :, :MLA_NOPE], ((0, 0), (0, 0), (0, LANE - MLA_NOPE)))
    wk = wk.reshape(MLA_KV_RANK, MLA_HEADS * LANE).astype(MXU_DTYPE)
    wv = kv[:, :, MLA_NOPE:].reshape(MLA_KV_RANK, MLA_HEADS * MLA_NOPE).astype(MXU_DTYPE)
    ob = w_out.reshape(MLA_HEADS, MLA_NOPE, -1)
    zo = jnp.zeros_like(ob)
    par = (jnp.arange(MLA_HEADS) % 2)[:, None, None]
    wo = jnp.concatenate([jnp.where(par == 0, ob, zo), jnp.where(par == 1, ob, zo)], axis=1)
    wo = wo.reshape(MLA_HEADS * LANE, -1).astype(MXU_DTYPE)
    return w1, wq, wk, wv, wo


def _segments(bounds):
    seg_id = np.zeros((LANE,), np.int32)
    for k, lo in enumerate(bounds):
        seg_id[lo:] = k
    return jnp.asarray(seg_id[:, None] == seg_id[None, :], dtype=MXU_DTYPE)


def _moe_set(xc, h_flat, aff_t, mods_i, weights, layer, *, n_tok, tok_off, tile_c, mod_row, tag):
    bsz, ttot, d = xc.shape
    wg, wu, wd = weights
    cap = EC_CAPACITY_FACTOR * n_tok // N_EXPERTS
    if n_tok % (8 * LANE) == 0:
        gates, pos, idxm = _route_blocked(aff_t, n_tok, tok_off, cap, "route_" + tag)
    else:
        bl = min(2 * LANE, n_tok)
        tri = jnp.asarray(np.triu(np.ones((bl, bl), np.float32)), dtype=MXU_DTYPE)
        gates, pos, idxm = _route(aff_t, tri, n_tok, tok_off // n_tok, cap, "route_" + tag)
    idx = jnp.swapaxes(idxm[:, :, :N_EXPERTS], 1, 2)
    rows = idx + (jnp.arange(bsz, dtype=I32) * ttot + tok_off)[:, None, None]
    tm = min(256, cap)
    rows = jnp.swapaxes(rows, 0, 1).reshape(N_EXPERTS * bsz * (cap // tm), 1, tm)
    ys = _experts(rows, h_flat, wg, wu, wd, layer, bsz, cap, tm, "experts_" + tag)
    w_dma = min(cap, tile_c + 2 * BF16_SUBLANES)
    starts = pos[:, :, ::tile_c]
    a0 = jnp.minimum((starts // BF16_SUBLANES) * BF16_SUBLANES, cap - w_dma)
    a0 = jnp.swapaxes(a0, 1, 2).reshape(-1).astype(I32)
    return _combine(a0, xc, mods_i, gates, pos, ys, tile_c, tok_off // tile_c, mod_row, w_dma, "combine_" + tag)


def kernel(x, c, ctx, c_ctx, w_ada, b_ada, norm_mix, norm_ffn, w_in_even, a_qk_norm, diff_lambda, a_subln,
           b_qk_norm, w_out_even, w_in_odd, mla_q_norm, w_q_up, mla_kv_norm, w_kv_up, mla_qk_norm, w_out_odd,
           w_router, w_exp_gate, w_exp_up, w_exp_down):
    bsz, t_lat, d = x.shape
    t_ctx = ctx.shape[1]
    ttot = t_lat + t_ctx
    assert d == D_MODEL and bsz < MOD_ROWS and t_lat % t_ctx == 0 and t_lat % GRID_W == 0
    tile = min(256, t_ctx)
    tk = min(512, t_lat)
    tq = min(1024, t_lat)
    n_lat_tiles = t_lat // tile

    xc = jnp.concatenate([x, ctx], axis=1)
    c_rows = jnp.zeros((MOD_ROWS, d), F32).at[:bsz].set(c).at[bsz].set(c_ctx)
    mods = _adaln(c_rows, w_ada, b_ada).reshape(DEPTH, MOD_ROWS, N_MOD, d)

    tabs_even = _rope_tables(t_lat, t_ctx, HEAD_DIM, 0, HEAD_DIM)
    tabs_odd = _rope_tables(t_lat, t_ctx, MLA_ROPE, MLA_NOPE, LANE)
    seg_even = _segments([0, HEAD_DIM])
    seg_odd = _segments([0, MLA_NOPE, MLA_NOPE + MLA_ROPE])
    invn_odd = jnp.concatenate([jnp.full((MLA_NOPE,), 1.0 / MLA_NOPE, F32),
                                jnp.full((LANE - MLA_NOPE,), 1.0 / MLA_ROPE, F32)])[None, :]
    expert_w = (w_exp_gate.astype(MXU_DTYPE), w_exp_up.astype(MXU_DTYPE), w_exp_down.astype(MXU_DTYPE))

    for i in range(DEPTH):
        last = i == DEPTH - 1
        j = i // 2
        mods_i = mods[i]
        g_mix = norm_mix[i][None, :]
        if i % 2 == 0:
            w1, wo = _even_weights(w_in_even[j], w_out_even[j])
            gains = _pad_rows([jnp.tile(a_qk_norm[j, 0], 2), jnp.tile(a_qk_norm[j, 1], 2),
                               jnp.tile(b_qk_norm[j, 0], 2), jnp.tile(b_qk_norm[j, 1], 2)])
            q, k, v, vt = _project(_proj_even_kernel, xc, mods_i, [g_mix, w1, seg_even, gains], tabs_even,
                               (16, 5, 5), tile, n_lat_tiles, "proj_even")
            kmap = lambda h: jnp.where(h < 8, h // 2, 4)
            o_lat, o_ctx = _attention(q, k, v, vt, kmap, kmap, t_lat, t_ctx, tq, tk, "attn_even")
            lam_init = 0.8 - 0.6 * math.exp(-0.3 * i)
            extra = [diff_lambda[j], a_subln[j][None, :]]
        else:
            w1, wq, wk, wv, wo = _odd_weights(w_in_odd[j], w_q_up[j], w_kv_up[j], w_out_odd[j])
            qk = mla_qk_norm[j]
            gains = _pad_rows([_lanes(qk[0]), _lanes(qk[1, :MLA_NOPE]), _lanes(qk[1, MLA_NOPE:], MLA_NOPE)])
            consts = [g_mix, w1, mla_q_norm[j][None, :], wq, mla_kv_norm[j][None, :], wk, wv, seg_odd,
                      invn_odd, gains]
            q, k, v, vt = _project(_proj_odd_kernel, xc, mods_i, consts, tabs_odd,
                               (MLA_HEADS, MLA_HEADS, MLA_HEADS // 2), tile, n_lat_tiles, "proj_odd")
            o_lat, o_ctx = _attention(q, k, v, vt, lambda h: h, lambda h: h // 2, t_lat, t_ctx, tq, tk, "attn_odd")
            lam_init = 0.0
            extra = []
        consts = [wo, norm_ffn[i][None, :], jnp.swapaxes(w_router[i], 0, 1)] + extra
        xc, h2, aff_t = _outproj(o_lat, o_ctx, xc, mods_i, consts, i % 2 == 0, lam_init, tile, n_lat_tiles,
                                 "outproj_even" if i % 2 == 0 else "outproj_odd")
        h_flat = h2.reshape(bsz * ttot, d)
        xc = _moe_set(xc, h_flat, aff_t, mods_i, expert_w, i, n_tok=t_lat, tok_off=0,
                      tile_c=min(LANE, t_lat), mod_row=None, tag="lat")
        if not last:
            xc = _moe_set(xc, h_flat, aff_t, mods_i, expert_w, i, n_tok=t_ctx, tok_off=t_lat,
                          tile_c=t_ctx, mod_row=bsz, tag="ctx")
    return xc[:, :t_lat]
```

```python
import functools
import math

import numpy as np
import jax
import jax.numpy as jnp
from jax import lax
from jax.experimental import pallas as pl
from jax.experimental.pallas import tpu as pltpu

F32 = jnp.float32
I32 = jnp.int32
MXU_DTYPE = jnp.bfloat16

D_MODEL = 1024
DEPTH = 4
GRID_W = 64
HEAD_DIM = 64
ROPE_THETA = 10000.0
EPS = 1e-6
N_MOD = 6
N_EXPERTS = 16
EC_CAPACITY_FACTOR = 2
MLA_HEADS = 16
MLA_NOPE = 64
MLA_ROPE = 32
MLA_Q_RANK = 256
MLA_KV_RANK = 128

LANE = 128
BF16_SUBLANES = 16
MOD_ROWS = 16
VMEM_LIMIT = 52 * 1024 * 1024
HIGHEST = lax.Precision.HIGHEST
LOG2_E = math.log2(math.e)
MAX_SOFTMAX_REFERENCE = 56.0
REFERENCE_SLACK = 1.004


def _dot(a, b):
    return jnp.dot(a, b, preferred_element_type=F32)


def _dot_nt(a, b, precision=None):
    return lax.dot_general(a, b, (((1,), (1,)), ((), ())), preferred_element_type=F32, precision=precision)


def _split_bf16(x):
    hi = x.astype(MXU_DTYPE)
    return hi, (x - hi.astype(F32)).astype(MXU_DTYPE)


def _params(sem, vmem=VMEM_LIMIT):
    return pltpu.CompilerParams(dimension_semantics=sem, vmem_limit_bytes=vmem)


def _norm_mod(x, g, shift, scale):
    ms = jnp.mean(x * x, axis=-1, keepdims=True)
    return (x * lax.rsqrt(ms + EPS)) * g * (1.0 + scale) + shift


def _mod_kernel(c_ref, w_ref, b_ref, o_ref):
    c = c_ref[...]
    sc = c * jax.nn.sigmoid(c)
    o_ref[0] = jnp.dot(sc, w_ref[0], preferred_element_type=F32, precision=HIGHEST) + b_ref[0]


def _adaln(c_rows, w_ada, b_ada):
    depth, d, n = w_ada.shape
    nb = n // 4
    return pl.pallas_call(
        _mod_kernel,
        out_shape=jax.ShapeDtypeStruct((depth, MOD_ROWS, n), F32),
        grid=(depth, n // nb),
        in_specs=[pl.BlockSpec((MOD_ROWS, d), lambda i, j: (0, 0)),
                  pl.BlockSpec((1, d, nb), lambda i, j: (i, 0, j)),
                  pl.BlockSpec((1, 1, nb), lambda i, j: (i, 0, j))],
        out_specs=pl.BlockSpec((1, MOD_ROWS, nb), lambda i, j: (i, 0, j)),
        compiler_params=_params(("arbitrary", "arbitrary")),
        name="adaln",
    )(c_rows, w_ada, b_ada.reshape(depth, 1, n))


def _seg_normrope(blk, seg, inv_n, gain, tabs, dist):
    ss = _dot((blk * blk).astype(MXU_DTYPE), seg)
    xn = blk * lax.rsqrt(ss * inv_n + EPS) * gain
    if tabs is not None:
        cos, sin_up, sin_dn = tabs
        xn = xn * cos + pltpu.roll(xn, dist, 1) * sin_up + pltpu.roll(xn, LANE - dist, 1) * sin_dn
    return xn


def _proj_even_kernel(x_ref, mod_ref, g_ref, w_ref, seg_ref, gains_ref, cos_ref, sup_ref, sdn_ref,
                      q_ref, k_ref, vt_ref):
    mod = mod_ref[0]
    h = _norm_mod(x_ref[0], g_ref[...], mod[0:1], mod[1:2]).astype(MXU_DTYPE)
    seg = seg_ref[...]
    tabs = (cos_ref[...], sup_ref[...], sdn_ref[...])
    inv_n = 1.0 / HEAD_DIM
    scale = HEAD_DIM ** -0.5 * LOG2_E
    lane = lax.broadcasted_iota(I32, (1, LANE), 1)
    first = (lane < HEAD_DIM).astype(F32)
    second = 1.0 - first
    dist = HEAD_DIM // 4

    y = _dot(h, w_ref[...])

    def blk(j):
        return y[:, LANE * j:LANE * (j + 1)]

    for hh in range(4):
        qa = _seg_normrope(blk(hh), seg, inv_n, gains_ref[0:1], tabs, dist) * scale
        q_ref[0, 2 * hh] = (qa * first).astype(q_ref.dtype)
        q_ref[0, 2 * hh + 1] = (qa * second).astype(q_ref.dtype)
        ka = _seg_normrope(blk(4 + hh), seg, inv_n, gains_ref[1:2], tabs, dist)
        k_ref[0, hh] = ka.astype(k_ref.dtype)
        vt_ref[0, hh] = blk(8 + hh).T.astype(vt_ref.dtype)
    for hh in range(8):
        qb = _seg_normrope(blk(12 + hh), seg, inv_n, gains_ref[2:3], tabs, dist) * scale
        q_ref[0, 8 + hh] = qb.astype(q_ref.dtype)
    kb = _seg_normrope(blk(20), seg, inv_n, gains_ref[3:4], tabs, dist)
    k_ref[0, 4] = kb.astype(k_ref.dtype)
    vt_ref[0, 4] = blk(21).T.astype(vt_ref.dtype)


def _proj_odd_kernel(x_ref, mod_ref, g_ref, w1_ref, qn_ref, wq_ref, kvn_ref, wk_ref, wv_ref, seg_ref,
                     invn_ref, gains_ref, cos_ref, sup_ref, sdn_ref, q_ref, k_ref, vt_ref):
    mod = mod_ref[0]
    h = _norm_mod(x_ref[0], g_ref[...], mod[0:1], mod[1:2]).astype(MXU_DTYPE)
    y1 = _dot(h, w1_ref[...])
    cq = y1[:, :MLA_Q_RANK]
    cqn = (cq * lax.rsqrt(jnp.mean(cq * cq, axis=-1, keepdims=True) + EPS) * qn_ref[...]).astype(MXU_DTYPE)
    ckv = y1[:, MLA_Q_RANK:MLA_Q_RANK + MLA_KV_RANK]
    ckvn = (ckv * lax.rsqrt(jnp.mean(ckv * ckv, axis=-1, keepdims=True) + EPS) * kvn_ref[...]).astype(MXU_DTYPE)
    qf = _dot(cqn, wq_ref[...])
    kf = _dot(ckvn, wk_ref[...])
    vf = _dot(ckvn, wv_ref[...])
    seg = seg_ref[...]
    inv_n = invn_ref[...]
    tabs = (cos_ref[...], sup_ref[...], sdn_ref[...])
    scale = (MLA_NOPE + MLA_ROPE) ** -0.5 * LOG2_E
    dist = MLA_ROPE // 4
    kr = _seg_normrope(y1[:, MLA_Q_RANK + MLA_KV_RANK:], seg, inv_n, gains_ref[2:3], tabs, dist)
    for hh in range(MLA_HEADS):
        qh = _seg_normrope(qf[:, LANE * hh:LANE * (hh + 1)], seg, inv_n, gains_ref[0:1], tabs, dist) * scale
        q_ref[0, hh] = qh.astype(q_ref.dtype)
        kh = _seg_normrope(kf[:, LANE * hh:LANE * (hh + 1)], seg, inv_n, gains_ref[1:2], None, 0) + kr
        k_ref[0, hh] = kh.astype(k_ref.dtype)
    for j in range(MLA_HEADS // 2):
        vt_ref[0, j] = vf[:, LANE * j:LANE * (j + 1)].T.astype(vt_ref.dtype)


def _full(shape):
    zeros = (0,) * len(shape)
    return pl.BlockSpec(shape, lambda b, j: zeros)


def _project(kernel_fn, xc, mods_i, consts, tabs, heads, tile, n_lat_tiles, name):
    bsz, ttot, d = xc.shape
    n_tiles = ttot // tile
    nq, nk, nv = heads

    def mod_map(b, j):
        return (jnp.where(j < n_lat_tiles, b, bsz), 0, 0)

    in_specs = [pl.BlockSpec((1, tile, d), lambda b, j: (b, j, 0)),
                pl.BlockSpec((1, N_MOD, d), mod_map)]
    in_specs += [_full(a.shape) for a in consts]
    in_specs += [pl.BlockSpec((tile, LANE), lambda b, j: (j, 0))] * 3
    out_shape = [jax.ShapeDtypeStruct((bsz, n, ttot, LANE), MXU_DTYPE) for n in (nq, nk)]
    out_specs = [pl.BlockSpec((1, n, tile, LANE), lambda b, j: (b, 0, j, 0)) for n in (nq, nk)]
    out_shape.append(jax.ShapeDtypeStruct((bsz, nv, LANE, ttot), MXU_DTYPE))
    out_specs.append(pl.BlockSpec((1, nv, LANE, tile), lambda b, j: (b, 0, 0, j)))
    return pl.pallas_call(
        kernel_fn, out_shape=out_shape, grid=(bsz, n_tiles), in_specs=in_specs, out_specs=out_specs,
        compiler_params=_params(("arbitrary", "arbitrary")), name=name,
    )(xc, mods_i, *consts, *tabs)


def _attn_kernel(q_ref, k_ref, vt_ref, o_ref, knorm_scr, *, chunks):
    q = q_ref[0, 0]
    tq = q.shape[0]
    dv = vt_ref.shape[2]

    def pad_lanes(out):
        if dv == LANE:
            return out
        return jnp.concatenate([out, jnp.zeros((tq, LANE - dv), F32)], axis=1)

    @pl.when(pl.program_id(2) == 0)
    def _():
        kmax2 = jnp.zeros((1, 1), F32)
        for lo, size in chunks:
            kf = k_ref[0, 0, lo:lo + size, :].astype(F32)
            r = jnp.sum(kf * kf, axis=-1, keepdims=True)
            kmax2 = jnp.maximum(kmax2, jnp.max(r, axis=0, keepdims=True))
        knorm_scr[...] = jnp.broadcast_to(jnp.sqrt(kmax2), knorm_scr.shape)

    qf = q.astype(F32)
    ref_row = jnp.sqrt(jnp.sum(qf * qf, axis=-1, keepdims=True)) * knorm_scr[0:1, 0:1]

    def fixed_reference(_):
        ones = jnp.ones((8, LANE), MXU_DTYPE)
        ref_t = jnp.sqrt(_dot_nt(ones, (qf * qf).astype(MXU_DTYPE))[0:1]) * (REFERENCE_SLACK * knorm_scr[0:1, 0:1])
        acc = jnp.zeros((dv + BF16_SUBLANES, tq), F32)
        for lo, size in chunks:
            k = k_ref[0, 0, lo:lo + size, :]
            vt = vt_ref[0, 0, :, lo:lo + size]
            lhs = jnp.concatenate([vt, jnp.ones((BF16_SUBLANES, size), vt.dtype)], axis=0)
            p = jnp.exp2(_dot_nt(k, q) - ref_t)
            acc = acc + _dot(lhs, p.astype(vt.dtype))
        out_t = acc[:dv] / acc[dv:dv + 1]
        return pad_lanes(out_t.T)

    def running_max(_):
        carry = (jnp.full((tq, 1), -1e30, F32), jnp.zeros((tq, 1), F32), jnp.zeros((tq, dv), F32))
        for lo, size in chunks:
            m, l, acc = carry
            s = _dot_nt(q, k_ref[0, 0, lo:lo + size, :])
            m_new = jnp.maximum(m, jnp.max(s, axis=-1, keepdims=True))
            alpha = jnp.exp2(m - m_new)
            p = jnp.exp2(s - m_new)
            l = alpha * l + jnp.sum(p, axis=-1, keepdims=True)
            vt = vt_ref[0, 0, :, lo:lo + size]
            carry = (m_new, l, alpha * acc + _dot_nt(p.astype(vt.dtype), vt))
        _, l, acc = carry
        return pad_lanes(acc / l)

    in_range = jnp.max(ref_row) < MAX_SOFTMAX_REFERENCE
    o_ref[0, 0] = lax.cond(in_range, fixed_reference, running_max, 0)


def _attention(q, k, vt, groups, t_lat, t_ctx, tq, tk, name):
    bsz, _, ttot, _ = q.shape
    chunks = [(c * tk, tk) for c in range(t_lat // tk)] + [(t_lat, t_ctx)]
    sem = ("arbitrary", "arbitrary", "arbitrary")
    scratch = [pltpu.VMEM((8, LANE), F32)]
    ctx_blk = t_lat // t_ctx
    outs = []
    for gi, (h0, nh, dv, kmap, vrow) in enumerate(groups):
        vt_view = vt.reshape(bsz, vt.shape[1] * LANE // dv, dv, ttot)
        o_lat = pl.pallas_call(
            functools.partial(_attn_kernel, chunks=chunks),
            out_shape=jax.ShapeDtypeStruct((bsz, nh, t_lat, LANE), F32), grid=(bsz, nh, t_lat // tq),
            in_specs=[pl.BlockSpec((1, 1, tq, LANE), lambda b, h, t, h0=h0: (b, h0 + h, t, 0)),
                      pl.BlockSpec((1, 1, ttot, LANE), lambda b, h, t, kmap=kmap: (b, kmap(h), 0, 0)),
                      pl.BlockSpec((1, 1, dv, ttot), lambda b, h, t, vrow=vrow: (b, vrow(h), 0, 0))],
            out_specs=pl.BlockSpec((1, 1, tq, LANE), lambda b, h, t: (b, h, t, 0)),
            scratch_shapes=scratch, compiler_params=_params(sem), name=f"{name}_{gi}",
        )(q, k, vt_view)
        o_ctx = pl.pallas_call(
            functools.partial(_attn_kernel, chunks=[(0, t_ctx)]),
            out_shape=jax.ShapeDtypeStruct((bsz, nh, t_ctx, LANE), F32), grid=(bsz, nh, 1),
            in_specs=[pl.BlockSpec((1, 1, t_ctx, LANE), lambda b, h, t, h0=h0: (b, h0 + h, ctx_blk, 0)),
                      pl.BlockSpec((1, 1, t_ctx, LANE), lambda b, h, t, kmap=kmap: (b, kmap(h), ctx_blk, 0)),
                      pl.BlockSpec((1, 1, dv, t_ctx), lambda b, h, t, vrow=vrow: (b, vrow(h), 0, ctx_blk))],
            out_specs=pl.BlockSpec((1, 1, t_ctx, LANE), lambda b, h, t: (b, h, 0, 0)),
            scratch_shapes=scratch, compiler_params=_params(sem), name=f"{name}_{gi}_ctx",
        )(q, k, vt_view)
        outs.append((o_lat, o_ctx))
    return outs


def _outproj_kernel(*refs, even, lam_init, n_lat_tiles, group_heads):
    o_refs, refs = refs[:2 * len(group_heads)], refs[2 * len(group_heads):]
    if even:
        x_ref, mod_ref, wout_ref, gffn_ref, wr_ref, lam_ref, subln_ref, xo_ref, h2_ref, aff_ref = refs
    else:
        x_ref, mod_ref, wout_ref, gffn_ref, wr_ref, xo_ref, h2_ref, aff_ref = refs
    mod = mod_ref[0]
    is_ctx = pl.program_id(1) >= n_lat_tiles

    def head(hh):
        for gi, nh in enumerate(group_heads):
            if hh < nh:
                return jnp.where(is_ctx, o_refs[2 * gi + 1][0, hh], o_refs[2 * gi][0, hh])
            hh -= nh

    if even:
        lv = lam_ref[...]
        lam = (jnp.exp(jnp.sum(lv[0:1] * lv[1:2], axis=-1, keepdims=True))
               - jnp.exp(jnp.sum(lv[2:3] * lv[3:4], axis=-1, keepdims=True)) + lam_init)
        parts = []
        for hh in range(4):
            oa = head(2 * hh) - lam * head(2 * hh + 1)
            oa = oa * lax.rsqrt(jnp.mean(oa * oa, axis=-1, keepdims=True) + EPS) * subln_ref[...]
            parts.append((oa * (1.0 - lam_init)).astype(MXU_DTYPE))
        for hh in range(8):
            parts.append(head(8 + hh).astype(MXU_DTYPE))
    else:
        parts = [head(hh).astype(MXU_DTYPE) for hh in range(MLA_HEADS)]
    y = _dot(jnp.concatenate(parts, axis=1), wout_ref[...])
    xn = x_ref[0] + mod[2:3] * y
    xo_ref[0] = xn
    h2 = _norm_mod(xn, gffn_ref[...], mod[3:4], mod[4:5])
    h2_ref[0] = h2
    w_hi, w_lo = _split_bf16(wr_ref[...])
    h_hi, h_lo = _split_bf16(h2)
    logits = _dot_nt(w_hi, h_hi) + (_dot_nt(w_hi, h_lo) + _dot_nt(w_lo, h_hi))
    e = jnp.exp(logits - jnp.max(logits, axis=0, keepdims=True))
    aff_ref[0] = e / jnp.sum(e, axis=0, keepdims=True)


def _outproj(o_groups, xc, mods_i, consts, even, lam_init, tile, n_lat_tiles, name):
    bsz, ttot, d = xc.shape

    def mod_map(b, j):
        return (jnp.where(j < n_lat_tiles, b, bsz), 0, 0)

    in_specs, o_args = [], []
    for o_lat, o_ctx in o_groups:
        nh = o_lat.shape[1]
        in_specs += [pl.BlockSpec((1, nh, tile, LANE), lambda b, j: (b, 0, jnp.minimum(j, n_lat_tiles - 1), 0)),
                     pl.BlockSpec((1, nh, tile, LANE), lambda b, j: (b, 0, jnp.maximum(j - n_lat_tiles, 0), 0))]
        o_args += [o_lat, o_ctx]
    in_specs += [pl.BlockSpec((1, tile, d), lambda b, j: (b, j, 0)),
                 pl.BlockSpec((1, N_MOD, d), mod_map)]
    in_specs += [_full(a.shape) for a in consts]
    out_shape = [jax.ShapeDtypeStruct((bsz, ttot, d), F32), jax.ShapeDtypeStruct((bsz, ttot, d), F32),
                 jax.ShapeDtypeStruct((bsz, N_EXPERTS, ttot), F32)]
    out_specs = [pl.BlockSpec((1, tile, d), lambda b, j: (b, j, 0)),
                 pl.BlockSpec((1, tile, d), lambda b, j: (b, j, 0)),
                 pl.BlockSpec((1, N_EXPERTS, tile), lambda b, j: (b, 0, j))]
    return pl.pallas_call(
        functools.partial(_outproj_kernel, even=even, lam_init=lam_init, n_lat_tiles=n_lat_tiles,
                          group_heads=tuple(o.shape[1] for o, _ in o_groups)),
        out_shape=out_shape, grid=(bsz, ttot // tile), in_specs=in_specs, out_specs=out_specs,
        compiler_params=_params(("arbitrary", "arbitrary")), name=name,
    )(*o_args, xc, mods_i, *consts)


def _cumsum_lanes(x, tri):
    bl = tri.shape[0]
    outs = []
    carry = jnp.zeros((x.shape[0], 1), F32)
    for c in range(x.shape[1] // bl):
        r = _dot(x[:, c * bl:(c + 1) * bl].astype(MXU_DTYPE), tri) + carry
        outs.append(r)
        carry = r[:, bl - 1:bl]
    return outs[0] if len(outs) == 1 else jnp.concatenate(outs, axis=1)


def _route_kernel(aff_ref, tri_ref, g_ref, pos_ref, idx_ref, cum_scr, *, cap, chunk):
    a = aff_ref[0]
    n_tok = a.shape[1]
    bits = pltpu.bitcast(a, I32)

    thr = _threshold_bits(bits, cap)
    gt = bits > thr
    eq = bits == thr
    need = cap - jnp.sum(gt.astype(F32), axis=1, keepdims=True)
    eqf = eq.astype(F32)
    tri = tri_ref[...]
    eq_rank = _cumsum_lanes(eqf, tri) - eqf
    sel = jnp.logical_or(gt, jnp.logical_and(eq, eq_rank < need))
    self = sel.astype(F32)
    cum = _cumsum_lanes(self, tri)
    g_ref[0] = jnp.where(sel, a, 0.0)
    pos_ref[0] = (cum - self).astype(I32)

    n_chunks = n_tok // chunk
    for c in range(n_chunks):
        cum_scr[c] = cum[:, c * chunk:(c + 1) * chunk]
    slot = lax.broadcasted_iota(I32, (cap, 1), 0).astype(F32)
    lane = lax.broadcasted_iota(I32, (1, LANE), 1)
    idxmat = jnp.zeros((cap, LANE), F32)
    for e in range(N_EXPERTS):
        def count(c, acc):
            return acc + jnp.where(cum_scr[c, e:e + 1, :] <= slot, 1.0, 0.0)

        acc = lax.fori_loop(0, n_chunks, count, jnp.zeros((cap, chunk), F32))
        idxmat = idxmat + jnp.sum(acc, axis=1, keepdims=True) * (lane == e).astype(F32)
    idx_ref[0] = idxmat.astype(I32)


def _threshold_bits(bits, cap):
    def search(i, cur):
        cand = cur | lax.shift_left(jnp.int32(1), 30 - i)
        cnt = jnp.sum((bits >= cand).astype(I32), axis=1, keepdims=True)
        return jnp.where(cnt >= cap, cand, cur)

    return lax.fori_loop(0, 31, search, jnp.zeros((bits.shape[0], 1), I32))


def _route_blocked_kernel(aff_ref, aff4_ref, tri_ref, low_ref, upp_ref, g_ref, pos_ref, idx_ref, *, cap):
    bits2 = pltpu.bitcast(aff_ref[0], I32)
    thr = _threshold_bits(bits2, cap)
    need = cap - jnp.sum((bits2 > thr).astype(F32), axis=1, keepdims=True)
    tri, low, upp = tri_ref[...], low_ref[...], upp_ref[...]
    nb = aff4_ref.shape[2]
    ones8 = jnp.ones((8, LANE), MXU_DTYPE)
    slot = lax.broadcasted_iota(I32, (cap, 1), 0).astype(F32)
    row_id = lax.broadcasted_iota(I32, (1, nb), 1).astype(F32)
    lane = lax.broadcasted_iota(I32, (1, LANE), 1)

    def prefix(x01):
        within = _dot(x01.astype(MXU_DTYPE), tri)
        tot = jnp.broadcast_to(within[:, LANE - 1:LANE], (nb, LANE)).astype(MXU_DTYPE)
        return within, _dot(low, tot)

    idxmat = jnp.zeros((cap, LANE), F32)
    for e in range(N_EXPERTS):
        a = aff4_ref[0, e]
        b = pltpu.bitcast(a, I32)
        t = thr[e:e + 1, :]
        gt = b > t
        eq = b == t
        eqf = eq.astype(F32)
        w_eq, off_eq = prefix(eqf)
        tie_ok = jnp.logical_and(eq, (w_eq + off_eq - eqf) < need[e:e + 1, :])
        sel = jnp.logical_or(gt, tie_ok)
        self = sel.astype(F32)
        within, rowoff = prefix(self)
        g_ref[0, e] = jnp.where(sel, a, 0.0)
        pos_ref[0, e] = (within + rowoff - self).astype(I32)
        tot_l = _dot_nt(ones8, self.astype(MXU_DTYPE))
        start_l = _dot(tot_l.astype(MXU_DTYPE), upp)[0:1]
        end_l = start_l + tot_l[0:1]
        inrow = jnp.where(start_l <= slot, jnp.where(slot < end_l, 1.0, 0.0), 0.0)
        local = _dot(inrow.astype(MXU_DTYPE), within.astype(MXU_DTYPE))
        rank = slot - jnp.sum(inrow * start_l, axis=1, keepdims=True)
        col = jnp.sum(jnp.where(local <= rank, 1.0, 0.0), axis=1, keepdims=True)
        row = jnp.sum(inrow * row_id, axis=1, keepdims=True)
        idxmat = idxmat + (row * LANE + col) * (lane == e).astype(F32)
    idx_ref[0] = idxmat.astype(I32)


def _route_blocked(aff_t, n_tok, tok_off, cap, name):
    bsz = aff_t.shape[0]
    nb = n_tok // LANE
    aff4 = aff_t[:, :, tok_off:tok_off + n_tok].reshape(bsz, N_EXPERTS, nb, LANE)
    tri = jnp.asarray(np.triu(np.ones((LANE, LANE), np.float32)), dtype=MXU_DTYPE)
    low = jnp.asarray(np.tril(np.ones((nb, nb), np.float32), -1), dtype=MXU_DTYPE)
    upp = jnp.asarray(np.triu(np.ones((nb, nb), np.float32), 1), dtype=MXU_DTYPE)
    blk4 = pl.BlockSpec((1, N_EXPERTS, nb, LANE), lambda b: (b, 0, 0, 0))
    g4, pos4, idxm = pl.pallas_call(
        functools.partial(_route_blocked_kernel, cap=cap),
        out_shape=[jax.ShapeDtypeStruct((bsz, N_EXPERTS, nb, LANE), F32),
                   jax.ShapeDtypeStruct((bsz, N_EXPERTS, nb, LANE), I32),
                   jax.ShapeDtypeStruct((bsz, cap, LANE), I32)],
        grid=(bsz,),
        in_specs=[pl.BlockSpec((1, N_EXPERTS, n_tok), lambda b: (b, 0, tok_off // n_tok)), blk4,
                  pl.BlockSpec(tri.shape, lambda b: (0, 0)), pl.BlockSpec(low.shape, lambda b: (0, 0)),
                  pl.BlockSpec(upp.shape, lambda b: (0, 0))],
        out_specs=[blk4, blk4, pl.BlockSpec((1, cap, LANE), lambda b: (b, 0, 0))],
        compiler_params=_params(("arbitrary",)), name=name,
    )(aff_t, aff4, tri, low, upp)
    return g4.reshape(bsz, N_EXPERTS, n_tok), pos4.reshape(bsz, N_EXPERTS, n_tok), idxm


def _route(aff_t, tri, n_tok, blk_off, cap, name):
    bsz = aff_t.shape[0]
    chunk = min(512, n_tok)
    kern = functools.partial(_route_kernel, cap=cap, chunk=chunk)
    return pl.pallas_call(
        kern,
        out_shape=[jax.ShapeDtypeStruct((bsz, N_EXPERTS, n_tok), F32),
                   jax.ShapeDtypeStruct((bsz, N_EXPERTS, n_tok), I32),
                   jax.ShapeDtypeStruct((bsz, cap, LANE), I32)],
        grid=(bsz,),
        in_specs=[pl.BlockSpec((1, N_EXPERTS, n_tok), lambda b: (b, 0, blk_off)),
                  pl.BlockSpec(tri.shape, lambda b: (0, 0))],
        out_specs=[pl.BlockSpec((1, N_EXPERTS, n_tok), lambda b: (b, 0, 0)),
                   pl.BlockSpec((1, N_EXPERTS, n_tok), lambda b: (b, 0, 0)),
                   pl.BlockSpec((1, cap, LANE), lambda b: (b, 0, 0))],
        scratch_shapes=[pltpu.VMEM((n_tok // chunk, N_EXPERTS, chunk), F32)],
        compiler_params=_params(("arbitrary",)), name=name,
    )(aff_t, tri)


def _expert_kernel(idx_ref, nxt_ref, h_hbm, wg_ref, wu_ref, wd_ref, y_ref, buf, sem):
    tm = buf.shape[1]
    step = (pl.program_id(0) * pl.num_programs(1) + pl.program_id(1)) * pl.num_programs(2) + pl.program_id(2)
    n_steps = pl.num_programs(0) * pl.num_programs(1) * pl.num_programs(2)
    cur = step % 2

    def gather(rows_ref, half):
        for r in range(tm):
            pltpu.make_async_copy(h_hbm.at[pl.ds(rows_ref[0, 0, r], 1), :], buf.at[half, pl.ds(r, 1), :],
                                  sem.at[half]).start()

    def wait(half):
        pltpu.make_async_copy(h_hbm.at[pl.ds(0, tm), :], buf.at[half], sem.at[half]).wait()

    @pl.when(step == 0)
    def _():
        gather(idx_ref, cur)

    gather(nxt_ref, 1 - cur)
    wait(cur)
    x = buf[cur].astype(MXU_DTYPE)
    a = _dot(x, wg_ref[0, 0])
    u = _dot(x, wu_ref[0, 0])
    hmid = (a * jax.nn.sigmoid(a) * u).astype(MXU_DTYPE)
    y_ref[0, 0] = _dot(hmid, wd_ref[0, 0]).astype(y_ref.dtype)

    @pl.when(step == n_steps - 1)
    def _():
        wait(1 - cur)


def _experts(rows, h_flat, wg, wu, wd, layer, bsz, cap, tm, name):
    d = h_flat.shape[1]
    nt = cap // tm
    last = N_EXPERTS * bsz * nt - 1
    wspec = pl.BlockSpec((1, 1, d, d), lambda e, b, t: (layer, e, 0, 0))

    def tile_id(e, b, t):
        return (e * bsz + b) * nt + t

    return pl.pallas_call(
        _expert_kernel,
        out_shape=jax.ShapeDtypeStruct((N_EXPERTS, bsz, cap, d), MXU_DTYPE),
        grid=(N_EXPERTS, bsz, nt),
        in_specs=[pl.BlockSpec((1, 1, tm), lambda e, b, t: (tile_id(e, b, t), 0, 0), memory_space=pltpu.SMEM),
                  pl.BlockSpec((1, 1, tm), lambda e, b, t: (jnp.minimum(tile_id(e, b, t) + 1, last), 0, 0),
                               memory_space=pltpu.SMEM),
                  pl.BlockSpec(memory_space=pl.ANY), wspec, wspec, wspec],
        out_specs=pl.BlockSpec((1, 1, tm, d), lambda e, b, t: (e, b, t, 0)),
        scratch_shapes=[pltpu.VMEM((2, tm, d), F32), pltpu.SemaphoreType.DMA((2,))],
        compiler_params=_params(("arbitrary", "arbitrary", "arbitrary")), name=name,
    )(rows, rows, h_flat, wg, wu, wd)


def _combine_kernel(a0_ref, x_ref, mod_ref, g_ref, pos_ref, ys_hbm, o_ref, buf, sem, *, w_dma, n_tiles):
    tile = x_ref.shape[1]
    kbuf = buf.shape[2]
    step = pl.program_id(0) * n_tiles + pl.program_id(1)
    n_steps = pl.num_programs(0) * n_tiles
    cur = step % 2
    base = step * N_EXPERTS

    def window(e, at_step, half):
        a0 = pl.multiple_of(a0_ref[at_step * N_EXPERTS + e], BF16_SUBLANES)
        return pltpu.make_async_copy(ys_hbm.at[e, at_step // n_tiles, pl.ds(a0, w_dma), :],
                                     buf.at[half, e, pl.ds(0, w_dma), :], sem.at[half])

    @pl.when(step == 0)
    def _():
        if w_dma < kbuf:
            buf[:, :, w_dma:, :] = jnp.zeros((2, N_EXPERTS, kbuf - w_dma, buf.shape[3]), buf.dtype)
        for e in range(N_EXPERTS):
            window(e, step, cur).start()

    @pl.when(step + 1 < n_steps)
    def _():
        for e in range(N_EXPERTS):
            window(e, step + 1, 1 - cur).start()

    sub = lax.broadcasted_iota(I32, (N_EXPERTS, 1), 0)
    a0v = jnp.zeros((N_EXPERTS, 1), I32)
    for e in range(N_EXPERTS):
        a0v = jnp.where(sub == e, a0_ref[base + e], a0v)
    rel = jnp.clip(pos_ref[0] - a0v, -1, kbuf) + 1
    eye = (lax.broadcasted_iota(I32, (tile, tile), 0) == lax.broadcasted_iota(I32, (tile, tile), 1)
           ).astype(MXU_DTYPE)

    def to_cols(m):
        return _dot_nt(eye, m.astype(MXU_DTYPE))

    rel_col = 16.0 * to_cols(rel >> 4) + to_cols(rel & 15) - 1.0
    g = g_ref[0]
    g_hi = g.astype(MXU_DTYPE)
    r1 = g - g_hi.astype(F32)
    g_mid = r1.astype(MXU_DTYPE)
    g_lo = (r1 - g_mid.astype(F32)).astype(MXU_DTYPE)
    g_col = _dot_nt(eye, g_hi) + _dot_nt(eye, g_mid) + _dot_nt(eye, g_lo)

    lane = lax.broadcasted_iota(I32, (1, kbuf), 1).astype(F32)
    acc = jnp.zeros((tile, x_ref.shape[2]), F32)
    pltpu.make_async_copy(ys_hbm.at[:, 0, pl.ds(0, w_dma), :], buf.at[cur, :, pl.ds(0, w_dma), :],
                          sem.at[cur]).wait()
    for e in range(N_EXPERTS):
        onehot = jnp.where(rel_col[:, e:e + 1] == lane, 1.0, 0.0).astype(MXU_DTYPE)
        acc = acc + g_col[:, e:e + 1] * _dot(onehot, buf[cur, e])
    o_ref[0] = x_ref[0] + mod_ref[0][5:6] * acc


def _combine(a0, xc, mods_i, gates, pos, ys, tile, blk_off, mod_row, w_dma, in_place, name):
    bsz, _, d = xc.shape
    n_tok = gates.shape[2]
    n_tiles = n_tok // tile
    kbuf = 2 * LANE

    def mod_map(b, j, a0_ref):
        return (b if mod_row is None else mod_row, 0, 0)

    grid_spec = pltpu.PrefetchScalarGridSpec(
        num_scalar_prefetch=1, grid=(bsz, n_tiles),
        in_specs=[pl.BlockSpec((1, tile, d), lambda b, j, a: (b, blk_off + j, 0)),
                  pl.BlockSpec((1, N_MOD, d), mod_map),
                  pl.BlockSpec((1, N_EXPERTS, tile), lambda b, j, a: (b, 0, j)),
                  pl.BlockSpec((1, N_EXPERTS, tile), lambda b, j, a: (b, 0, j)),
                  pl.BlockSpec(memory_space=pl.ANY)],
        out_specs=pl.BlockSpec((1, tile, d), lambda b, j, a: (b, (blk_off if in_place else 0) + j, 0)),
        scratch_shapes=[pltpu.VMEM((2, N_EXPERTS, kbuf, d), MXU_DTYPE), pltpu.SemaphoreType.DMA((2,))])
    return pl.pallas_call(
        functools.partial(_combine_kernel, w_dma=w_dma, n_tiles=n_tiles),
        out_shape=jax.ShapeDtypeStruct(xc.shape if in_place else (bsz, n_tok, d), F32), grid_spec=grid_spec,
        input_output_aliases={1: 0} if in_place else {},
        compiler_params=_params(("arbitrary", "arbitrary")), name=name,
    )(a0, xc, mods_i, gates, pos, ys)


def _rope_tables(t_lat, t_ctx, rot_dim, lane_off, period):
    half = rot_dim // 2
    quarter = rot_dim // 4
    t = np.arange(t_lat)
    freqs = ROPE_THETA ** (-np.arange(0, half, 2, dtype=np.float32) / half)
    ang_row = jnp.asarray((t // GRID_W).astype(np.float32))[:, None] * jnp.asarray(freqs)
    ang_col = jnp.asarray((t % GRID_W).astype(np.float32))[:, None] * jnp.asarray(freqs)
    lane = np.arange(LANE)
    u = (lane - lane_off) % period
    active = (lane >= lane_off) & (u < rot_dim)
    is_col = (u // half) == 1
    w = u % half
    fidx = w % quarter
    first = w < quarter
    ang = jnp.where(jnp.asarray(is_col)[None, :], ang_col[:, fidx], ang_row[:, fidx])
    act = jnp.asarray(active)[None, :]
    cos = jnp.where(act, jnp.cos(ang), 1.0)
    sin = jnp.where(act, jnp.sin(ang), 0.0)
    sin_up = jnp.where(jnp.asarray(~first)[None, :], sin, 0.0)
    sin_dn = jnp.where(jnp.asarray(first)[None, :], -sin, 0.0)
    pad = lambda a, v: jnp.concatenate([a, jnp.full((t_ctx, LANE), v, F32)], axis=0)
    return pad(cos.astype(F32), 1.0), pad(sin_up.astype(F32), 0.0), pad(sin_dn.astype(F32), 0.0)


def _lanes(vec, offset=0):
    return jnp.zeros((LANE,), F32).at[offset:offset + vec.shape[0]].set(vec)


def _pad_rows(rows):
    out = jnp.zeros((8, LANE), F32)
    return out.at[:len(rows)].set(jnp.stack(rows))


def _even_weights(w_in, w_out):
    d = w_in.shape[0]
    qb = w_in[:, 1536:2048].reshape(d, 8, HEAD_DIM)
    z = jnp.zeros_like(qb)
    g = (jnp.arange(8) // 4)[None, :, None]
    qb_pad = jnp.concatenate([jnp.where(g == 0, qb, z), jnp.where(g == 1, qb, z)], axis=-1).reshape(d, 8 * LANE)
    w1 = jnp.concatenate([w_in[:, :1536], qb_pad, w_in[:, 2048:]], axis=1).astype(MXU_DTYPE)
    ob = w_out[512:].reshape(8, HEAD_DIM, -1)
    ob_pad = jnp.concatenate([ob, jnp.zeros_like(ob)], axis=1).reshape(8 * LANE, -1)
    wo = jnp.concatenate([w_out[:512], ob_pad], axis=0).astype(MXU_DTYPE)
    return w1, wo


def _odd_weights(w_in, w_q_up, w_kv_up, w_out):
    d = w_in.shape[0]
    nk = MLA_Q_RANK + MLA_KV_RANK
    w1 = jnp.zeros((d, nk + LANE), F32).at[:, :nk].set(w_in[:, :nk])
    w1 = w1.at[:, nk + MLA_NOPE:nk + MLA_NOPE + MLA_ROPE].set(w_in[:, nk:]).astype(MXU_DTYPE)
    dq = MLA_NOPE + MLA_ROPE
    wq = jnp.pad(w_q_up.reshape(MLA_Q_RANK, MLA_HEADS, dq), ((0, 0), (0, 0), (0, LANE - dq)))
    wq = wq.reshape(MLA_Q_RANK, MLA_HEADS * LANE).astype(MXU_DTYPE)
    kv = w_kv_up.reshape(MLA_KV_RANK, MLA_HEADS, 2 * MLA_NOPE)
    wk = jnp.pad(kv[:, :, :MLA_NOPE], ((0, 0), (0, 0), (0, LANE - MLA_NOPE)))
    wk = wk.reshape(MLA_KV_RANK, MLA_HEADS * LANE).astype(MXU_DTYPE)
    wv = kv[:, :, MLA_NOPE:].reshape(MLA_KV_RANK, MLA_HEADS * MLA_NOPE).astype(MXU_DTYPE)
    ob = w_out.reshape(MLA_HEADS, MLA_NOPE, -1)
    wo = jnp.concatenate([ob, jnp.zeros_like(ob)], axis=1)
    wo = wo.reshape(MLA_HEADS * LANE, -1).astype(MXU_DTYPE)
    return w1, wq, wk, wv, wo


def _segments(bounds):
    seg_id = np.zeros((LANE,), np.int32)
    for k, lo in enumerate(bounds):
        seg_id[lo:] = k
    return jnp.asarray(seg_id[:, None] == seg_id[None, :], dtype=MXU_DTYPE)


def _moe_set(xc, h_flat, aff_t, mods_i, weights, layer, *, n_tok, tok_off, tile_c, mod_row, tag, in_place=True):
    bsz, ttot, d = xc.shape
    wg, wu, wd = weights
    cap = EC_CAPACITY_FACTOR * n_tok // N_EXPERTS
    if n_tok % (8 * LANE) == 0:
        gates, pos, idxm = _route_blocked(aff_t, n_tok, tok_off, cap, "route_" + tag)
    else:
        bl = min(2 * LANE, n_tok)
        tri = jnp.asarray(np.triu(np.ones((bl, bl), np.float32)), dtype=MXU_DTYPE)
        gates, pos, idxm = _route(aff_t, tri, n_tok, tok_off // n_tok, cap, "route_" + tag)
    idx = jnp.swapaxes(idxm[:, :, :N_EXPERTS], 1, 2)
    rows = idx + (jnp.arange(bsz, dtype=I32) * ttot + tok_off)[:, None, None]
    tm = min(256, cap)
    rows = jnp.swapaxes(rows, 0, 1).reshape(N_EXPERTS * bsz * (cap // tm), 1, tm)
    ys = _experts(rows, h_flat, wg, wu, wd, layer, bsz, cap, tm, "experts_" + tag)
    w_dma = min(cap, tile_c + 2 * BF16_SUBLANES)
    starts = pos[:, :, ::tile_c]
    a0 = jnp.minimum((starts // BF16_SUBLANES) * BF16_SUBLANES, cap - w_dma)
    a0 = jnp.swapaxes(a0, 1, 2).reshape(-1).astype(I32)
    return _combine(a0, xc, mods_i, gates, pos, ys, tile_c, tok_off // tile_c, mod_row, w_dma, in_place,
                    "combine_" + tag)


def kernel(x, c, ctx, c_ctx, w_ada, b_ada, norm_mix, norm_ffn, w_in_even, a_qk_norm, diff_lambda, a_subln,
           b_qk_norm, w_out_even, w_in_odd, mla_q_norm, w_q_up, mla_kv_norm, w_kv_up, mla_qk_norm, w_out_odd,
           w_router, w_exp_gate, w_exp_up, w_exp_down):
    bsz, t_lat, d = x.shape
    t_ctx = ctx.shape[1]
    ttot = t_lat + t_ctx
    assert d == D_MODEL and bsz < MOD_ROWS and t_lat % t_ctx == 0 and t_lat % GRID_W == 0
    tile = min(256, t_ctx)
    tk = min(512, t_lat)
    tq = min(1024, t_lat)
    n_lat_tiles = t_lat // tile

    xc = jnp.concatenate([x, ctx], axis=1)
    c_rows = jnp.zeros((MOD_ROWS, d), F32).at[:bsz].set(c).at[bsz].set(c_ctx)
    mods = _adaln(c_rows, w_ada, b_ada).reshape(DEPTH, MOD_ROWS, N_MOD, d)

    tabs_even = _rope_tables(t_lat, t_ctx, HEAD_DIM, 0, HEAD_DIM)
    tabs_odd = _rope_tables(t_lat, t_ctx, MLA_ROPE, MLA_NOPE, LANE)
    seg_even = _segments([0, HEAD_DIM])
    seg_odd = _segments([0, MLA_NOPE, MLA_NOPE + MLA_ROPE])
    invn_odd = jnp.concatenate([jnp.full((MLA_NOPE,), 1.0 / MLA_NOPE, F32),
                                jnp.full((LANE - MLA_NOPE,), 1.0 / MLA_ROPE, F32)])[None, :]
    expert_w = (w_exp_gate.astype(MXU_DTYPE), w_exp_up.astype(MXU_DTYPE), w_exp_down.astype(MXU_DTYPE))

    for i in range(DEPTH):
        last = i == DEPTH - 1
        j = i // 2
        mods_i = mods[i]
        g_mix = norm_mix[i][None, :]
        if i % 2 == 0:
            w1, wo = _even_weights(w_in_even[j], w_out_even[j])
            gains = _pad_rows([jnp.tile(a_qk_norm[j, 0], 2), jnp.tile(a_qk_norm[j, 1], 2),
                               jnp.tile(b_qk_norm[j, 0], 2), jnp.tile(b_qk_norm[j, 1], 2)])
            q, k, vt = _project(_proj_even_kernel, xc, mods_i, [g_mix, w1, seg_even, gains], tabs_even,
                                (16, 5, 5), tile, n_lat_tiles, "proj_even")
            groups = [(0, 8, LANE, lambda h: h // 2, lambda h: h // 2),
                      (8, 8, HEAD_DIM, lambda h: 4, lambda h: 8 + h // 4)]
            o_groups = _attention(q, k, vt, groups, t_lat, t_ctx, tq, tk, "attn_even")
            lam_init = 0.8 - 0.6 * math.exp(-0.3 * i)
            extra = [diff_lambda[j], a_subln[j][None, :]]
        else:
            w1, wq, wk, wv, wo = _odd_weights(w_in_odd[j], w_q_up[j], w_kv_up[j], w_out_odd[j])
            qk = mla_qk_norm[j]
            gains = _pad_rows([_lanes(qk[0]), _lanes(qk[1, :MLA_NOPE]), _lanes(qk[1, MLA_NOPE:], MLA_NOPE)])
            consts = [g_mix, w1, mla_q_norm[j][None, :], wq, mla_kv_norm[j][None, :], wk, wv, seg_odd,
                      invn_odd, gains]
            q, k, vt = _project(_proj_odd_kernel, xc, mods_i, consts, tabs_odd,
                                (MLA_HEADS, MLA_HEADS, MLA_HEADS // 2), tile, n_lat_tiles, "proj_odd")
            groups = [(0, MLA_HEADS, MLA_NOPE, lambda h: h, lambda h: h)]
            o_groups = _attention(q, k, vt, groups, t_lat, t_ctx, tq, tk, "attn_odd")
            lam_init = 0.0
            extra = []
        consts = [wo, norm_ffn[i][None, :], jnp.swapaxes(w_router[i], 0, 1)] + extra
        xc, h2, aff_t = _outproj(o_groups, xc, mods_i, consts, i % 2 == 0, lam_init, tile, n_lat_tiles,
                                 "outproj_even" if i % 2 == 0 else "outproj_odd")
        h_flat = h2.reshape(bsz * ttot, d)
        xc = _moe_set(xc, h_flat, aff_t, mods_i, expert_w, i, n_tok=t_lat, tok_off=0,
                      tile_c=min(LANE, t_lat), mod_row=None, tag="lat", in_place=not last)
        if not last:
            xc = _moe_set(xc, h_flat, aff_t, mods_i, expert_w, i, n_tok=t_ctx, tok_off=t_lat,
                          tile_c=t_ctx, mod_row=bsz, tag="ctx")
    return xc
```

```python
import functools
import math

import numpy as np
import jax
import jax.numpy as jnp
from jax import lax
from jax.experimental import pallas as pl
from jax.experimental.pallas import tpu as pltpu

F32 = jnp.float32
I32 = jnp.int32
MXU_DTYPE = jnp.bfloat16

D_MODEL = 1024
DEPTH = 4
GRID_W = 64
HEAD_DIM = 64
ROPE_THETA = 10000.0
EPS = 1e-6
N_MOD = 6
N_EXPERTS = 16
EC_CAPACITY_FACTOR = 2
MLA_HEADS = 16
MLA_NOPE = 64
MLA_ROPE = 32
MLA_Q_RANK = 256
MLA_KV_RANK = 128

LANE = 128
BF16_SUBLANES = 16
MOD_ROWS = 16
VMEM_LIMIT = 52 * 1024 * 1024
HIGHEST = lax.Precision.HIGHEST
LOG2_E = math.log2(math.e)
MAX_SOFTMAX_REFERENCE = 56.0
REFERENCE_SLACK = 1.004


def _dot(a, b):
    return jnp.dot(a, b, preferred_element_type=F32)


def _dot_nt(a, b, precision=None):
    return lax.dot_general(a, b, (((1,), (1,)), ((), ())), preferred_element_type=F32, precision=precision)


def _split_bf16(x):
    hi = x.astype(MXU_DTYPE)
    return hi, (x - hi.astype(F32)).astype(MXU_DTYPE)


def _params(sem, vmem=VMEM_LIMIT):
    return pltpu.CompilerParams(dimension_semantics=sem, vmem_limit_bytes=vmem)


def _norm_mod(x, g, shift, scale):
    ms = jnp.mean(x * x, axis=-1, keepdims=True)
    return (x * lax.rsqrt(ms + EPS)) * g * (1.0 + scale) + shift


def _mod_kernel(c_ref, w_ref, b_ref, o_ref):
    c = c_ref[...]
    sc = c * jax.nn.sigmoid(c)
    o_ref[0] = jnp.dot(sc, w_ref[0], preferred_element_type=F32, precision=HIGHEST) + b_ref[0]


def _adaln(c_rows, w_ada, b_ada):
    depth, d, n = w_ada.shape
    nb = n // 4
    return pl.pallas_call(
        _mod_kernel,
        out_shape=jax.ShapeDtypeStruct((depth, MOD_ROWS, n), F32),
        grid=(depth, n // nb),
        in_specs=[pl.BlockSpec((MOD_ROWS, d), lambda i, j: (0, 0)),
                  pl.BlockSpec((1, d, nb), lambda i, j: (i, 0, j)),
                  pl.BlockSpec((1, 1, nb), lambda i, j: (i, 0, j))],
        out_specs=pl.BlockSpec((1, MOD_ROWS, nb), lambda i, j: (i, 0, j)),
        compiler_params=_params(("arbitrary", "arbitrary")),
        name="adaln",
    )(c_rows, w_ada, b_ada.reshape(depth, 1, n))


def _seg_normrope(blk, seg, inv_n, gain, tabs, dist):
    ss = _dot((blk * blk).astype(MXU_DTYPE), seg)
    xn = blk * lax.rsqrt(ss * inv_n + EPS) * gain
    if tabs is not None:
        cos, sin_up, sin_dn = tabs
        xn = xn * cos + pltpu.roll(xn, dist, 1) * sin_up + pltpu.roll(xn, LANE - dist, 1) * sin_dn
    return xn


def _proj_even_kernel(x_ref, mod_ref, g_ref, w_ref, seg_ref, gains_ref, cos_ref, sup_ref, sdn_ref,
                      q_ref, k_ref, vt_ref):
    mod = mod_ref[0]
    h = _norm_mod(x_ref[0], g_ref[...], mod[0:1], mod[1:2]).astype(MXU_DTYPE)
    seg = seg_ref[...]
    tabs = (cos_ref[...], sup_ref[...], sdn_ref[...])
    inv_n = 1.0 / HEAD_DIM
    scale = HEAD_DIM ** -0.5 * LOG2_E
    lane = lax.broadcasted_iota(I32, (1, LANE), 1)
    first = (lane < HEAD_DIM).astype(F32)
    second = 1.0 - first
    dist = HEAD_DIM // 4

    y = _dot(h, w_ref[...])

    def blk(j):
        return y[:, LANE * j:LANE * (j + 1)]

    for hh in range(4):
        qa = _seg_normrope(blk(hh), seg, inv_n, gains_ref[0:1], tabs, dist) * scale
        q_ref[0, 2 * hh] = (qa * first).astype(q_ref.dtype)
        q_ref[0, 2 * hh + 1] = (qa * second).astype(q_ref.dtype)
        ka = _seg_normrope(blk(4 + hh), seg, inv_n, gains_ref[1:2], tabs, dist)
        k_ref[0, hh] = ka.astype(k_ref.dtype)
        vt_ref[0, hh] = blk(8 + hh).T.astype(vt_ref.dtype)
    for hh in range(8):
        qb = _seg_normrope(blk(12 + hh), seg, inv_n, gains_ref[2:3], tabs, dist) * scale
        q_ref[0, 8 + hh] = qb.astype(q_ref.dtype)
    kb = _seg_normrope(blk(20), seg, inv_n, gains_ref[3:4], tabs, dist)
    k_ref[0, 4] = kb.astype(k_ref.dtype)
    vt_ref[0, 4] = blk(21).T.astype(vt_ref.dtype)


def _proj_odd_kernel(x_ref, mod_ref, g_ref, w1_ref, qn_ref, wq_ref, kvn_ref, wk_ref, wv_ref, seg_ref,
                     invn_ref, gains_ref, cos_ref, sup_ref, sdn_ref, q_ref, k_ref, vt_ref):
    mod = mod_ref[0]
    h = _norm_mod(x_ref[0], g_ref[...], mod[0:1], mod[1:2]).astype(MXU_DTYPE)
    y1 = _dot(h, w1_ref[...])
    cq = y1[:, :MLA_Q_RANK]
    cqn = (cq * lax.rsqrt(jnp.mean(cq * cq, axis=-1, keepdims=True) + EPS) * qn_ref[...]).astype(MXU_DTYPE)
    ckv = y1[:, MLA_Q_RANK:MLA_Q_RANK + MLA_KV_RANK]
    ckvn = (ckv * lax.rsqrt(jnp.mean(ckv * ckv, axis=-1, keepdims=True) + EPS) * kvn_ref[...]).astype(MXU_DTYPE)
    qf = _dot(cqn, wq_ref[...])
    kf = _dot(ckvn, wk_ref[...])
    vf = _dot(ckvn, wv_ref[...])
    seg = seg_ref[...]
    inv_n = invn_ref[...]
    tabs = (cos_ref[...], sup_ref[...], sdn_ref[...])
    scale = (MLA_NOPE + MLA_ROPE) ** -0.5 * LOG2_E
    dist = MLA_ROPE // 4
    kr = _seg_normrope(y1[:, MLA_Q_RANK + MLA_KV_RANK:], seg, inv_n, gains_ref[2:3], tabs, dist)
    for hh in range(MLA_HEADS):
        qh = _seg_normrope(qf[:, LANE * hh:LANE * (hh + 1)], seg, inv_n, gains_ref[0:1], tabs, dist) * scale
        q_ref[0, hh] = qh.astype(q_ref.dtype)
        kh = _seg_normrope(kf[:, LANE * hh:LANE * (hh + 1)], seg, inv_n, gains_ref[1:2], None, 0) + kr
        k_ref[0, hh] = kh.astype(k_ref.dtype)
    for j in range(MLA_HEADS // 2):
        vt_ref[0, j] = vf[:, LANE * j:LANE * (j + 1)].T.astype(vt_ref.dtype)


def _full(shape):
    zeros = (0,) * len(shape)
    return pl.BlockSpec(shape, lambda b, j: zeros)


def _project(kernel_fn, xc, mods_i, consts, tabs, heads, tile, n_lat_tiles, name):
    bsz, ttot, d = xc.shape
    n_tiles = ttot // tile
    nq, nk, nv = heads

    def mod_map(b, j):
        return (jnp.where(j < n_lat_tiles, b, bsz), 0, 0)

    in_specs = [pl.BlockSpec((1, tile, d), lambda b, j: (b, j, 0)),
                pl.BlockSpec((1, N_MOD, d), mod_map)]
    in_specs += [_full(a.shape) for a in consts]
    in_specs += [pl.BlockSpec((tile, LANE), lambda b, j: (j, 0))] * 3
    out_shape = [jax.ShapeDtypeStruct((bsz, n, ttot, LANE), MXU_DTYPE) for n in (nq, nk)]
    out_specs = [pl.BlockSpec((1, n, tile, LANE), lambda b, j: (b, 0, j, 0)) for n in (nq, nk)]
    out_shape.append(jax.ShapeDtypeStruct((bsz, nv, LANE, ttot), MXU_DTYPE))
    out_specs.append(pl.BlockSpec((1, nv, LANE, tile), lambda b, j: (b, 0, 0, j)))
    return pl.pallas_call(
        kernel_fn, out_shape=out_shape, grid=(bsz, n_tiles), in_specs=in_specs, out_specs=out_specs,
        compiler_params=_params(("arbitrary", "arbitrary")), name=name,
    )(xc, mods_i, *consts, *tabs)


def _attn_kernel(q_ref, k_ref, vt_ref, o_ref, knorm_scr, *, chunks):
    q = q_ref[0, 0]
    tq = q.shape[0]
    dv = vt_ref.shape[2]

    def pad_lanes(out):
        if dv == LANE:
            return out
        return jnp.concatenate([out, jnp.zeros((tq, LANE - dv), F32)], axis=1)

    @pl.when(pl.program_id(2) == 0)
    def _():
        kmax2 = jnp.zeros((1, 1), F32)
        for lo, size in chunks:
            kf = k_ref[0, 0, lo:lo + size, :].astype(F32)
            r = jnp.sum(kf * kf, axis=-1, keepdims=True)
            kmax2 = jnp.maximum(kmax2, jnp.max(r, axis=0, keepdims=True))
        knorm_scr[...] = jnp.broadcast_to(jnp.sqrt(kmax2), knorm_scr.shape)

    qf = q.astype(F32)
    ref_row = jnp.sqrt(jnp.sum(qf * qf, axis=-1, keepdims=True)) * knorm_scr[0:1, 0:1]

    def fixed_reference(_):
        ones = jnp.ones((8, LANE), MXU_DTYPE)
        ref_t = jnp.sqrt(_dot_nt(ones, (qf * qf).astype(MXU_DTYPE))[0:1]) * (REFERENCE_SLACK * knorm_scr[0:1, 0:1])
        acc = jnp.zeros((dv + BF16_SUBLANES, tq), F32)
        for lo, size in chunks:
            k = k_ref[0, 0, lo:lo + size, :]
            vt = vt_ref[0, 0, :, lo:lo + size]
            lhs = jnp.concatenate([vt, jnp.ones((BF16_SUBLANES, size), vt.dtype)], axis=0)
            p = jnp.exp2(_dot_nt(k, q) - ref_t)
            acc = acc + _dot(lhs, p.astype(vt.dtype))
        out_t = acc[:dv] / acc[dv:dv + 1]
        return pad_lanes(out_t.T)

    def running_max(_):
        carry = (jnp.full((tq, 1), -1e30, F32), jnp.zeros((tq, 1), F32), jnp.zeros((tq, dv), F32))
        for lo, size in chunks:
            m, l, acc = carry
            s = _dot_nt(q, k_ref[0, 0, lo:lo + size, :])
            m_new = jnp.maximum(m, jnp.max(s, axis=-1, keepdims=True))
            alpha = jnp.exp2(m - m_new)
            p = jnp.exp2(s - m_new)
            l = alpha * l + jnp.sum(p, axis=-1, keepdims=True)
            vt = vt_ref[0, 0, :, lo:lo + size]
            carry = (m_new, l, alpha * acc + _dot_nt(p.astype(vt.dtype), vt))
        _, l, acc = carry
        return pad_lanes(acc / l)

    in_range = jnp.max(ref_row) < MAX_SOFTMAX_REFERENCE
    o_ref[0, 0] = lax.cond(in_range, fixed_reference, running_max, 0)


def _attention(q, k, vt, groups, t_lat, t_ctx, tq, tk, name):
    bsz, _, ttot, _ = q.shape
    chunks = [(c * tk, tk) for c in range(t_lat // tk)] + [(t_lat, t_ctx)]
    sem = ("arbitrary", "arbitrary", "arbitrary")
    scratch = [pltpu.VMEM((8, LANE), F32)]
    ctx_blk = t_lat // t_ctx
    outs = []
    for gi, (h0, nh, dv, kmap, vrow) in enumerate(groups):
        vt_view = vt.reshape(bsz, vt.shape[1] * LANE // dv, dv, ttot)
        o_lat = pl.pallas_call(
            functools.partial(_attn_kernel, chunks=chunks),
            out_shape=jax.ShapeDtypeStruct((bsz, nh, t_lat, LANE), F32), grid=(bsz, nh, t_lat // tq),
            in_specs=[pl.BlockSpec((1, 1, tq, LANE), lambda b, h, t, h0=h0: (b, h0 + h, t, 0)),
                      pl.BlockSpec((1, 1, ttot, LANE), lambda b, h, t, kmap=kmap: (b, kmap(h), 0, 0)),
                      pl.BlockSpec((1, 1, dv, ttot), lambda b, h, t, vrow=vrow: (b, vrow(h), 0, 0))],
            out_specs=pl.BlockSpec((1, 1, tq, LANE), lambda b, h, t: (b, h, t, 0)),
            scratch_shapes=scratch, compiler_params=_params(sem), name=f"{name}_{gi}",
        )(q, k, vt_view)
        o_ctx = pl.pallas_call(
            functools.partial(_attn_kernel, chunks=[(0, t_ctx)]),
            out_shape=jax.ShapeDtypeStruct((bsz, nh, t_ctx, LANE), F32), grid=(bsz, nh, 1),
            in_specs=[pl.BlockSpec((1, 1, t_ctx, LANE), lambda b, h, t, h0=h0: (b, h0 + h, ctx_blk, 0)),
                      pl.BlockSpec((1, 1, t_ctx, LANE), lambda b, h, t, kmap=kmap: (b, kmap(h), ctx_blk, 0)),
                      pl.BlockSpec((1, 1, dv, t_ctx), lambda b, h, t, vrow=vrow: (b, vrow(h), 0, ctx_blk))],
            out_specs=pl.BlockSpec((1, 1, t_ctx, LANE), lambda b, h, t: (b, h, 0, 0)),
            scratch_shapes=scratch, compiler_params=_params(sem), name=f"{name}_{gi}_ctx",
        )(q, k, vt_view)
        outs.append((o_lat, o_ctx))
    return outs


def _outproj_kernel(*refs, even, lam_init, n_lat_tiles, group_heads):
    o_refs, refs = refs[:2 * len(group_heads)], refs[2 * len(group_heads):]
    if even:
        x_ref, mod_ref, wout_ref, gffn_ref, wr_ref, lam_ref, subln_ref, xo_ref, h2_ref, aff_ref = refs
    else:
        x_ref, mod_ref, wout_ref, gffn_ref, wr_ref, xo_ref, h2_ref, aff_ref = refs
    mod = mod_ref[0]
    is_ctx = pl.program_id(1) >= n_lat_tiles

    def head(hh):
        for gi, nh in enumerate(group_heads):
            if hh < nh:
                return jnp.where(is_ctx, o_refs[2 * gi + 1][0, hh], o_refs[2 * gi][0, hh])
            hh -= nh

    if even:
        lv = lam_ref[...]
        lam = (jnp.exp(jnp.sum(lv[0:1] * lv[1:2], axis=-1, keepdims=True))
               - jnp.exp(jnp.sum(lv[2:3] * lv[3:4], axis=-1, keepdims=True)) + lam_init)
        parts = []
        for hh in range(4):
            oa = head(2 * hh) - lam * head(2 * hh + 1)
            oa = oa * lax.rsqrt(jnp.mean(oa * oa, axis=-1, keepdims=True) + EPS) * subln_ref[...]
            parts.append((oa * (1.0 - lam_init)).astype(MXU_DTYPE))
        for hh in range(8):
            parts.append(head(8 + hh).astype(MXU_DTYPE))
    else:
        parts = [head(hh).astype(MXU_DTYPE) for hh in range(MLA_HEADS)]
    y = _dot(jnp.concatenate(parts, axis=1), wout_ref[...])
    xn = x_ref[0] + mod[2:3] * y
    xo_ref[0] = xn
    h2 = _norm_mod(xn, gffn_ref[...], mod[3:4], mod[4:5])
    h2_ref[0] = h2
    w_hi, w_lo = _split_bf16(wr_ref[...])
    h_hi, h_lo = _split_bf16(h2)
    logits = _dot_nt(w_hi, h_hi) + (_dot_nt(w_hi, h_lo) + _dot_nt(w_lo, h_hi))
    e = jnp.exp(logits - jnp.max(logits, axis=0, keepdims=True))
    aff_ref[0] = e / jnp.sum(e, axis=0, keepdims=True)


def _outproj(o_groups, xc, mods_i, consts, even, lam_init, tile, n_lat_tiles, name):
    bsz, ttot, d = xc.shape

    def mod_map(b, j):
        return (jnp.where(j < n_lat_tiles, b, bsz), 0, 0)

    in_specs, o_args = [], []
    for o_lat, o_ctx in o_groups:
        nh = o_lat.shape[1]
        in_specs += [pl.BlockSpec((1, nh, tile, LANE), lambda b, j: (b, 0, jnp.minimum(j, n_lat_tiles - 1), 0)),
                     pl.BlockSpec((1, nh, tile, LANE), lambda b, j: (b, 0, jnp.maximum(j - n_lat_tiles, 0), 0))]
        o_args += [o_lat, o_ctx]
    in_specs += [pl.BlockSpec((1, tile, d), lambda b, j: (b, j, 0)),
                 pl.BlockSpec((1, N_MOD, d), mod_map)]
    in_specs += [_full(a.shape) for a in consts]
    out_shape = [jax.ShapeDtypeStruct((bsz, ttot, d), F32), jax.ShapeDtypeStruct((bsz, ttot, d), F32),
                 jax.ShapeDtypeStruct((bsz, N_EXPERTS, ttot), F32)]
    out_specs = [pl.BlockSpec((1, tile, d), lambda b, j: (b, j, 0)),
                 pl.BlockSpec((1, tile, d), lambda b, j: (b, j, 0)),
                 pl.BlockSpec((1, N_EXPERTS, tile), lambda b, j: (b, 0, j))]
    return pl.pallas_call(
        functools.partial(_outproj_kernel, even=even, lam_init=lam_init, n_lat_tiles=n_lat_tiles,
                          group_heads=tuple(o.shape[1] for o, _ in o_groups)),
        out_shape=out_shape, grid=(bsz, ttot // tile), in_specs=in_specs, out_specs=out_specs,
        compiler_params=_params(("arbitrary", "arbitrary")), name=name,
    )(*o_args, xc, mods_i, *consts)


def _cumsum_lanes(x, tri):
    bl = tri.shape[0]
    outs = []
    carry = jnp.zeros((x.shape[0], 1), F32)
    for c in range(x.shape[1] // bl):
        r = _dot(x[:, c * bl:(c + 1) * bl].astype(MXU_DTYPE), tri) + carry
        outs.append(r)
        carry = r[:, bl - 1:bl]
    return outs[0] if len(outs) == 1 else jnp.concatenate(outs, axis=1)


def _route_kernel(aff_ref, tri_ref, g_ref, pos_ref, idx_ref, cum_scr, *, cap, chunk):
    a = aff_ref[0]
    n_tok = a.shape[1]
    bits = pltpu.bitcast(a, I32)

    thr = _threshold_bits(bits, cap)
    gt = bits > thr
    eq = bits == thr
    need = cap - jnp.sum(gt.astype(F32), axis=1, keepdims=True)
    eqf = eq.astype(F32)
    tri = tri_ref[...]
    eq_rank = _cumsum_lanes(eqf, tri) - eqf
    sel = jnp.logical_or(gt, jnp.logical_and(eq, eq_rank < need))
    self = sel.astype(F32)
    cum = _cumsum_lanes(self, tri)
    g_ref[0] = jnp.where(sel, a, 0.0)
    pos_ref[0] = (cum - self).astype(I32)

    n_chunks = n_tok // chunk
    for c in range(n_chunks):
        cum_scr[c] = cum[:, c * chunk:(c + 1) * chunk]
    slot = lax.broadcasted_iota(I32, (cap, 1), 0).astype(F32)
    lane = lax.broadcasted_iota(I32, (1, LANE), 1)
    idxmat = jnp.zeros((cap, LANE), F32)
    for e in range(N_EXPERTS):
        def count(c, acc):
            return acc + jnp.where(cum_scr[c, e:e + 1, :] <= slot, 1.0, 0.0)

        acc = lax.fori_loop(0, n_chunks, count, jnp.zeros((cap, chunk), F32))
        idxmat = idxmat + jnp.sum(acc, axis=1, keepdims=True) * (lane == e).astype(F32)
    idx_ref[0] = idxmat.astype(I32)


def _threshold_bits(bits, cap):
    def search(i, cur):
        cand = cur | lax.shift_left(jnp.int32(1), 30 - i)
        cnt = jnp.sum((bits >= cand).astype(I32), axis=1, keepdims=True)
        return jnp.where(cnt >= cap, cand, cur)

    return lax.fori_loop(0, 31, search, jnp.zeros((bits.shape[0], 1), I32))


def _route_blocked_kernel(aff_ref, aff4_ref, tri_ref, low_ref, upp_ref, g_ref, pos_ref, idx_ref, *, cap):
    bits2 = pltpu.bitcast(aff_ref[0], I32)
    thr = _threshold_bits(bits2, cap)
    need = cap - jnp.sum((bits2 > thr).astype(F32), axis=1, keepdims=True)
    tri, low, upp = tri_ref[...], low_ref[...], upp_ref[...]
    nb = aff4_ref.shape[2]
    ones8 = jnp.ones((8, LANE), MXU_DTYPE)
    slot = lax.broadcasted_iota(I32, (cap, 1), 0).astype(F32)
    row_id = lax.broadcasted_iota(I32, (1, nb), 1).astype(F32)
    lane = lax.broadcasted_iota(I32, (1, LANE), 1)

    def prefix(x01):
        within = _dot(x01.astype(MXU_DTYPE), tri)
        tot = jnp.broadcast_to(within[:, LANE - 1:LANE], (nb, LANE)).astype(MXU_DTYPE)
        return within, _dot(low, tot)

    idxmat = jnp.zeros((cap, LANE), F32)
    for e in range(N_EXPERTS):
        a = aff4_ref[0, e]
        b = pltpu.bitcast(a, I32)
        t = thr[e:e + 1, :]
        gt = b > t
        eq = b == t
        eqf = eq.astype(F32)
        w_eq, off_eq = prefix(eqf)
        tie_ok = jnp.logical_and(eq, (w_eq + off_eq - eqf) < need[e:e + 1, :])
        sel = jnp.logical_or(gt, tie_ok)
        self = sel.astype(F32)
        within, rowoff = prefix(self)
        g_ref[0, e] = jnp.where(sel, a, 0.0)
        pos_ref[0, e] = (within + rowoff - self).astype(I32)
        tot_l = _dot_nt(ones8, self.astype(MXU_DTYPE))
        start_l = _dot(tot_l.astype(MXU_DTYPE), upp)[0:1]
        end_l = start_l + tot_l[0:1]
        inrow = jnp.where(start_l <= slot, jnp.where(slot < end_l, 1.0, 0.0), 0.0)
        local = _dot(inrow.astype(MXU_DTYPE), within.astype(MXU_DTYPE))
        rank = slot - jnp.sum(inrow * start_l, axis=1, keepdims=True)
        col = jnp.sum(jnp.where(local <= rank, 1.0, 0.0), axis=1, keepdims=True)
        row = jnp.sum(inrow * row_id, axis=1, keepdims=True)
        idxmat = idxmat + (row * LANE + col) * (lane == e).astype(F32)
    idx_ref[0] = idxmat.astype(I32)


def _route_blocked(aff_t, n_tok, tok_off, cap, name):
    bsz = aff_t.shape[0]
    nb = n_tok // LANE
    aff4 = aff_t[:, :, tok_off:tok_off + n_tok].reshape(bsz, N_EXPERTS, nb, LANE)
    tri = jnp.asarray(np.triu(np.ones((LANE, LANE), np.float32)), dtype=MXU_DTYPE)
    low = jnp.asarray(np.tril(np.ones((nb, nb), np.float32), -1), dtype=MXU_DTYPE)
    upp = jnp.asarray(np.triu(np.ones((nb, nb), np.float32), 1), dtype=MXU_DTYPE)
    blk4 = pl.BlockSpec((1, N_EXPERTS, nb, LANE), lambda b: (b, 0, 0, 0))
    g4, pos4, idxm = pl.pallas_call(
        functools.partial(_route_blocked_kernel, cap=cap),
        out_shape=[jax.ShapeDtypeStruct((bsz, N_EXPERTS, nb, LANE), F32),
                   jax.ShapeDtypeStruct((bsz, N_EXPERTS, nb, LANE), I32),
                   jax.ShapeDtypeStruct((bsz, cap, LANE), I32)],
        grid=(bsz,),
        in_specs=[pl.BlockSpec((1, N_EXPERTS, n_tok), lambda b: (b, 0, tok_off // n_tok)), blk4,
                  pl.BlockSpec(tri.shape, lambda b: (0, 0)), pl.BlockSpec(low.shape, lambda b: (0, 0)),
                  pl.BlockSpec(upp.shape, lambda b: (0, 0))],
        out_specs=[blk4, blk4, pl.BlockSpec((1, cap, LANE), lambda b: (b, 0, 0))],
        compiler_params=_params(("arbitrary",)), name=name,
    )(aff_t, aff4, tri, low, upp)
    return g4.reshape(bsz, N_EXPERTS, n_tok), pos4.reshape(bsz, N_EXPERTS, n_tok), idxm


def _route(aff_t, tri, n_tok, blk_off, cap, name):
    bsz = aff_t.shape[0]
    chunk = min(512, n_tok)
    kern = functools.partial(_route_kernel, cap=cap, chunk=chunk)
    return pl.pallas_call(
        kern,
        out_shape=[jax.ShapeDtypeStruct((bsz, N_EXPERTS, n_tok), F32),
                   jax.ShapeDtypeStruct((bsz, N_EXPERTS, n_tok), I32),
                   jax.ShapeDtypeStruct((bsz, cap, LANE), I32)],
        grid=(bsz,),
        in_specs=[pl.BlockSpec((1, N_EXPERTS, n_tok), lambda b: (b, 0, blk_off)),
                  pl.BlockSpec(tri.shape, lambda b: (0, 0))],
        out_specs=[pl.BlockSpec((1, N_EXPERTS, n_tok), lambda b: (b, 0, 0)),
                   pl.BlockSpec((1, N_EXPERTS, n_tok), lambda b: (b, 0, 0)),
                   pl.BlockSpec((1, cap, LANE), lambda b: (b, 0, 0))],
        scratch_shapes=[pltpu.VMEM((n_tok // chunk, N_EXPERTS, chunk), F32)],
        compiler_params=_params(("arbitrary",)), name=name,
    )(aff_t, tri)


def _expert_kernel(idx_ref, nxt_ref, h_hbm, wg_ref, wu_ref, wd_ref, y_ref, buf, sem):
    tm = buf.shape[1]
    step = (pl.program_id(0) * pl.num_programs(1) + pl.program_id(1)) * pl.num_programs(2) + pl.program_id(2)
    n_steps = pl.num_programs(0) * pl.num_programs(1) * pl.num_programs(2)
    cur = step % 2

    def gather(rows_ref, half):
        for r in range(tm):
            pltpu.make_async_copy(h_hbm.at[pl.ds(rows_ref[0, 0, r], 1), :], buf.at[half, pl.ds(r, 1), :],
                                  sem.at[half]).start()

    def wait(half):
        pltpu.make_async_copy(h_hbm.at[pl.ds(0, tm), :], buf.at[half], sem.at[half]).wait()

    @pl.when(step == 0)
    def _():
        gather(idx_ref, cur)

    gather(nxt_ref, 1 - cur)
    wait(cur)
    x = buf[cur].astype(MXU_DTYPE)
    a = _dot(x, wg_ref[0, 0])
    u = _dot(x, wu_ref[0, 0])
    hmid = (a * jax.nn.sigmoid(a) * u).astype(MXU_DTYPE)
    y_ref[0, 0] = _dot(hmid, wd_ref[0, 0]).astype(y_ref.dtype)

    @pl.when(step == n_steps - 1)
    def _():
        wait(1 - cur)


def _experts(rows, h_flat, wg, wu, wd, layer, bsz, cap, tm, name):
    d = h_flat.shape[1]
    nt = cap // tm
    last = N_EXPERTS * bsz * nt - 1
    wspec = pl.BlockSpec((1, 1, d, d), lambda e, b, t: (layer, e, 0, 0))

    def tile_id(e, b, t):
        return (e * bsz + b) * nt + t

    return pl.pallas_call(
        _expert_kernel,
        out_shape=jax.ShapeDtypeStruct((N_EXPERTS, bsz, cap, d), MXU_DTYPE),
        grid=(N_EXPERTS, bsz, nt),
        in_specs=[pl.BlockSpec((1, 1, tm), lambda e, b, t: (tile_id(e, b, t), 0, 0), memory_space=pltpu.SMEM),
                  pl.BlockSpec((1, 1, tm), lambda e, b, t: (jnp.minimum(tile_id(e, b, t) + 1, last), 0, 0),
                               memory_space=pltpu.SMEM),
                  pl.BlockSpec(memory_space=pl.ANY), wspec, wspec, wspec],
        out_specs=pl.BlockSpec((1, 1, tm, d), lambda e, b, t: (e, b, t, 0)),
        scratch_shapes=[pltpu.VMEM((2, tm, d), F32), pltpu.SemaphoreType.DMA((2,))],
        compiler_params=_params(("arbitrary", "arbitrary", "arbitrary")), name=name,
    )(rows, rows, h_flat, wg, wu, wd)


def _combine_kernel(a0_ref, x_ref, mod_ref, g_ref, pos_ref, ys_hbm, o_ref, buf, sem, *, w_dma, n_tiles):
    tile = x_ref.shape[1]
    kbuf = buf.shape[2]
    step = pl.program_id(0) * n_tiles + pl.program_id(1)
    n_steps = pl.num_programs(0) * n_tiles
    cur = step % 2
    base = step * N_EXPERTS

    def window(e, at_step, half):
        a0 = pl.multiple_of(a0_ref[at_step * N_EXPERTS + e], BF16_SUBLANES)
        return pltpu.make_async_copy(ys_hbm.at[e, at_step // n_tiles, pl.ds(a0, w_dma), :],
                                     buf.at[half, e, pl.ds(0, w_dma), :], sem.at[half])

    @pl.when(step == 0)
    def _():
        if w_dma < kbuf:
            buf[:, :, w_dma:, :] = jnp.zeros((2, N_EXPERTS, kbuf - w_dma, buf.shape[3]), buf.dtype)
        for e in range(N_EXPERTS):
            window(e, step, cur).start()

    @pl.when(step + 1 < n_steps)
    def _():
        for e in range(N_EXPERTS):
            window(e, step + 1, 1 - cur).start()

    sub = lax.broadcasted_iota(I32, (N_EXPERTS, 1), 0)
    a0v = jnp.zeros((N_EXPERTS, 1), I32)
    for e in range(N_EXPERTS):
        a0v = jnp.where(sub == e, a0_ref[base + e], a0v)
    rel = jnp.clip(pos_ref[0] - a0v, -1, kbuf) + 1
    eye = (lax.broadcasted_iota(I32, (tile, tile), 0) == lax.broadcasted_iota(I32, (tile, tile), 1)
           ).astype(MXU_DTYPE)

    def to_cols(m):
        return _dot_nt(eye, m.astype(MXU_DTYPE))

    rel_col = 16.0 * to_cols(rel >> 4) + to_cols(rel & 15) - 1.0
    g = g_ref[0]
    g_hi = g.astype(MXU_DTYPE)
    r1 = g - g_hi.astype(F32)
    g_mid = r1.astype(MXU_DTYPE)
    g_lo = (r1 - g_mid.astype(F32)).astype(MXU_DTYPE)
    g_col = _dot_nt(eye, g_hi) + _dot_nt(eye, g_mid) + _dot_nt(eye, g_lo)

    lane = lax.broadcasted_iota(I32, (1, kbuf), 1).astype(F32)
    acc = jnp.zeros((tile, x_ref.shape[2]), F32)
    pltpu.make_async_copy(ys_hbm.at[:, 0, pl.ds(0, w_dma), :], buf.at[cur, :, pl.ds(0, w_dma), :],
                          sem.at[cur]).wait()
    for e in range(N_EXPERTS):
        onehot = jnp.where(rel_col[:, e:e + 1] == lane, 1.0, 0.0).astype(MXU_DTYPE)
        acc = acc + g_col[:, e:e + 1] * _dot(onehot, buf[cur, e])
    o_ref[0] = x_ref[0] + mod_ref[0][5:6] * acc


def _combine(a0, xc, mods_i, gates, pos, ys, tile, blk_off, mod_row, w_dma, in_place, name):
    bsz, _, d = xc.shape
    n_tok = gates.shape[2]
    n_tiles = n_tok // tile
    kbuf = 2 * LANE

    def mod_map(b, j, a0_ref):
        return (b if mod_row is None else mod_row, 0, 0)

    grid_spec = pltpu.PrefetchScalarGridSpec(
        num_scalar_prefetch=1, grid=(bsz, n_tiles),
        in_specs=[pl.BlockSpec((1, tile, d), lambda b, j, a: (b, blk_off + j, 0)),
                  pl.BlockSpec((1, N_MOD, d), mod_map),
                  pl.BlockSpec((1, N_EXPERTS, tile), lambda b, j, a: (b, 0, j)),
                  pl.BlockSpec((1, N_EXPERTS, tile), lambda b, j, a: (b, 0, j)),
                  pl.BlockSpec(memory_space=pl.ANY)],
        out_specs=pl.BlockSpec((1, tile, d), lambda b, j, a: (b, (blk_off if in_place else 0) + j, 0)),
        scratch_shapes=[pltpu.VMEM((2, N_EXPERTS, kbuf, d), MXU_DTYPE), pltpu.SemaphoreType.DMA((2,))])
    return pl.pallas_call(
        functools.partial(_combine_kernel, w_dma=w_dma, n_tiles=n_tiles),
        out_shape=jax.ShapeDtypeStruct(xc.shape if in_place else (bsz, n_tok, d), F32), grid_spec=grid_spec,
        input_output_aliases={1: 0} if in_place else {},
        compiler_params=_params(("arbitrary", "arbitrary")), name=name,
    )(a0, xc, mods_i, gates, pos, ys)


def _rope_tables(t_lat, t_ctx, rot_dim, lane_off, period):
    half = rot_dim // 2
    quarter = rot_dim // 4
    t = np.arange(t_lat)
    freqs = ROPE_THETA ** (-np.arange(0, half, 2, dtype=np.float32) / half)
    ang_row = jnp.asarray((t // GRID_W).astype(np.float32))[:, None] * jnp.asarray(freqs)
    ang_col = jnp.asarray((t % GRID_W).astype(np.float32))[:, None] * jnp.asarray(freqs)
    lane = np.arange(LANE)
    u = (lane - lane_off) % period
    active = (lane >= lane_off) & (u < rot_dim)
    is_col = (u // half) == 1
    w = u % half
    fidx = w % quarter
    first = w < quarter
    ang = jnp.where(jnp.asarray(is_col)[None, :], ang_col[:, fidx], ang_row[:, fidx])
    act = jnp.asarray(active)[None, :]
    cos = jnp.where(act, jnp.cos(ang), 1.0)
    sin = jnp.where(act, jnp.sin(ang), 0.0)
    sin_up = jnp.where(jnp.asarray(~first)[None, :], sin, 0.0)
    sin_dn = jnp.where(jnp.asarray(first)[None, :], -sin, 0.0)
    pad = lambda a, v: jnp.concatenate([a, jnp.full((t_ctx, LANE), v, F32)], axis=0)
    return pad(cos.astype(F32), 1.0), pad(sin_up.astype(F32), 0.0), pad(sin_dn.astype(F32), 0.0)


def _lanes(vec, offset=0):
    return jnp.zeros((LANE,), F32).at[offset:offset + vec.shape[0]].set(vec)


def _pad_rows(rows):
    out = jnp.zeros((8, LANE), F32)
    return out.at[:len(rows)].set(jnp.stack(rows))


def _even_weights(w_in, w_out):
    d = w_in.shape[0]
    qb = w_in[:, 1536:2048].reshape(d, 8, HEAD_DIM)
    z = jnp.zeros_like(qb)
    g = (jnp.arange(8) // 4)[None, :, None]
    qb_pad = jnp.concatenate([jnp.where(g == 0, qb, z), jnp.where(g == 1, qb, z)], axis=-1).reshape(d, 8 * LANE)
    w1 = jnp.concatenate([w_in[:, :1536], qb_pad, w_in[:, 2048:]], axis=1).astype(MXU_DTYPE)
    ob = w_out[512:].reshape(8, HEAD_DIM, -1)
    zo = jnp.zeros_like(ob)
    go = (jnp.arange(8) // 4)[:, None, None]
    ob_pad = jnp.concatenate([jnp.where(go == 0, ob, zo), jnp.where(go == 1, ob, zo)], axis=1).reshape(8 * LANE, -1)
    wo = jnp.concatenate([w_out[:512], ob_pad], axis=0).astype(MXU_DTYPE)
    return w1, wo


def _odd_weights(w_in, w_q_up, w_kv_up, w_out):
    d = w_in.shape[0]
    nk = MLA_Q_RANK + MLA_KV_RANK
    w1 = jnp.zeros((d, nk + LANE), F32).at[:, :nk].set(w_in[:, :nk])
    w1 = w1.at[:, nk + MLA_NOPE:nk + MLA_NOPE + MLA_ROPE].set(w_in[:, nk:]).astype(MXU_DTYPE)
    dq = MLA_NOPE + MLA_ROPE
    wq = jnp.pad(w_q_up.reshape(MLA_Q_RANK, MLA_HEADS, dq), ((0, 0), (0, 0), (0, LANE - dq)))
    wq = wq.reshape(MLA_Q_RANK, MLA_HEADS * LANE).astype(MXU_DTYPE)
    kv = w_kv_up.reshape(MLA_KV_RANK, MLA_HEADS, 2 * MLA_NOPE)
    wk = jnp.pad(kv[:, :, :MLA_NOPE], ((0, 0), (0, 0), (0, LANE - MLA_NOPE)))
    wk = wk.reshape(MLA_KV_RANK, MLA_HEADS * LANE).astype(MXU_DTYPE)
    wv = kv[:, :, MLA_NOPE:].reshape(MLA_KV_RANK, MLA_HEADS * MLA_NOPE).astype(MXU_DTYPE)
    ob = w_out.reshape(MLA_HEADS, MLA_NOPE, -1)
    zo = jnp.zeros_like(ob)
    par = (jnp.arange(MLA_HEADS) % 2)[:, None, None]
    wo = jnp.concatenate([jnp.where(par == 0, ob, zo), jnp.where(par == 1, ob, zo)], axis=1)
    wo = wo.reshape(MLA_HEADS * LANE, -1).astype(MXU_DTYPE)
    return w1, wq, wk, wv, wo


def _segments(bounds):
    seg_id = np.zeros((LANE,), np.int32)
    for k, lo in enumerate(bounds):
        seg_id[lo:] = k
    return jnp.asarray(seg_id[:, None] == seg_id[None, :], dtype=MXU_DTYPE)


def _moe_set(xc, h_flat, aff_t, mods_i, weights, layer, *, n_tok, tok_off, tile_c, mod_row, tag, in_place=True):
    bsz, ttot, d = xc.shape
    wg, wu, wd = weights
    cap = EC_CAPACITY_FACTOR * n_tok // N_EXPERTS
    if n_tok % (8 * LANE) == 0:
        gates, pos, idxm = _route_blocked(aff_t, n_tok, tok_off, cap, "route_" + tag)
    else:
        bl = min(2 * LANE, n_tok)
        tri = jnp.asarray(np.triu(np.ones((bl, bl), np.float32)), dtype=MXU_DTYPE)
        gates, pos, idxm = _route(aff_t, tri, n_tok, tok_off // n_tok, cap, "route_" + tag)
    idx = jnp.swapaxes(idxm[:, :, :N_EXPERTS], 1, 2)
    rows = idx + (jnp.arange(bsz, dtype=I32) * ttot + tok_off)[:, None, None]
    tm = min(256, cap)
    rows = jnp.swapaxes(rows, 0, 1).reshape(N_EXPERTS * bsz * (cap // tm), 1, tm)
    ys = _experts(rows, h_flat, wg, wu, wd, layer, bsz, cap, tm, "experts_" + tag)
    w_dma = min(cap, tile_c + 2 * BF16_SUBLANES)
    starts = pos[:, :, ::tile_c]
    a0 = jnp.minimum((starts // BF16_SUBLANES) * BF16_SUBLANES, cap - w_dma)
    a0 = jnp.swapaxes(a0, 1, 2).reshape(-1).astype(I32)
    return _combine(a0, xc, mods_i, gates, pos, ys, tile_c, tok_off // tile_c, mod_row, w_dma, in_place,
                    "combine_" + tag)


def kernel(x, c, ctx, c_ctx, w_ada, b_ada, norm_mix, norm_ffn, w_in_even, a_qk_norm, diff_lambda, a_subln,
           b_qk_norm, w_out_even, w_in_odd, mla_q_norm, w_q_up, mla_kv_norm, w_kv_up, mla_qk_norm, w_out_odd,
           w_router, w_exp_gate, w_exp_up, w_exp_down):
    bsz, t_lat, d = x.shape
    t_ctx = ctx.shape[1]
    ttot = t_lat + t_ctx
    assert d == D_MODEL and bsz < MOD_ROWS and t_lat % t_ctx == 0 and t_lat % GRID_W == 0
    tile = min(256, t_ctx)
    tk = min(512, t_lat)
    tq = min(1024, t_lat)
    n_lat_tiles = t_lat // tile

    xc = jnp.concatenate([x, ctx], axis=1)
    c_rows = jnp.zeros((MOD_ROWS, d), F32).at[:bsz].set(c).at[bsz].set(c_ctx)
    mods = _adaln(c_rows, w_ada, b_ada).reshape(DEPTH, MOD_ROWS, N_MOD, d)

    tabs_even = _rope_tables(t_lat, t_ctx, HEAD_DIM, 0, HEAD_DIM)
    tabs_odd = _rope_tables(t_lat, t_ctx, MLA_ROPE, MLA_NOPE, LANE)
    seg_even = _segments([0, HEAD_DIM])
    seg_odd = _segments([0, MLA_NOPE, MLA_NOPE + MLA_ROPE])
    invn_odd = jnp.concatenate([jnp.full((MLA_NOPE,), 1.0 / MLA_NOPE, F32),
                                jnp.full((LANE - MLA_NOPE,), 1.0 / MLA_ROPE, F32)])[None, :]
    expert_w = (w_exp_gate.astype(MXU_DTYPE), w_exp_up.astype(MXU_DTYPE), w_exp_down.astype(MXU_DTYPE))

    for i in range(DEPTH):
        last = i == DEPTH - 1
        j = i // 2
        mods_i = mods[i]
        g_mix = norm_mix[i][None, :]
        if i % 2 == 0:
            w1, wo = _even_weights(w_in_even[j], w_out_even[j])
            gains = _pad_rows([jnp.tile(a_qk_norm[j, 0], 2), jnp.tile(a_qk_norm[j, 1], 2),
                               jnp.tile(b_qk_norm[j, 0], 2), jnp.tile(b_qk_norm[j, 1], 2)])
            q, k, vt = _project(_proj_even_kernel, xc, mods_i, [g_mix, w1, seg_even, gains], tabs_even,
                                (16, 5, 5), tile, n_lat_tiles, "proj_even")
            kmap = lambda h: jnp.where(h < 8, h // 2, 4)
            groups = [(0, 16, LANE, kmap, kmap)]
            o_groups = _attention(q, k, vt, groups, t_lat, t_ctx, tq, tk, "attn_even")
            lam_init = 0.8 - 0.6 * math.exp(-0.3 * i)
            extra = [diff_lambda[j], a_subln[j][None, :]]
        else:
            w1, wq, wk, wv, wo = _odd_weights(w_in_odd[j], w_q_up[j], w_kv_up[j], w_out_odd[j])
            qk = mla_qk_norm[j]
            gains = _pad_rows([_lanes(qk[0]), _lanes(qk[1, :MLA_NOPE]), _lanes(qk[1, MLA_NOPE:], MLA_NOPE)])
            consts = [g_mix, w1, mla_q_norm[j][None, :], wq, mla_kv_norm[j][None, :], wk, wv, seg_odd,
                      invn_odd, gains]
            q, k, vt = _project(_proj_odd_kernel, xc, mods_i, consts, tabs_odd,
                                (MLA_HEADS, MLA_HEADS, MLA_HEADS // 2), tile, n_lat_tiles, "proj_odd")
            groups = [(0, MLA_HEADS, LANE, lambda h: h, lambda h: h // 2)]
            o_groups = _attention(q, k, vt, groups, t_lat, t_ctx, tq, tk, "attn_odd")
            lam_init = 0.0
            extra = []
        consts = [wo, norm_ffn[i][None, :], jnp.swapaxes(w_router[i], 0, 1)] + extra
        xc, h2, aff_t = _outproj(o_groups, xc, mods_i, consts, i % 2 == 0, lam_init, tile, n_lat_tiles,
                                 "outproj_even" if i % 2 == 0 else "outproj_odd")
        h_flat = h2.reshape(bsz * ttot, d)
        xc = _moe_set(xc, h_flat, aff_t, mods_i, expert_w, i, n_tok=t_lat, tok_off=0,
                      tile_c=min(LANE, t_lat), mod_row=None, tag="lat", in_place=not last)
        if not last:
            xc = _moe_set(xc, h_flat, aff_t, mods_i, expert_w, i, n_tok=t_ctx, tok_off=t_lat,
                          tile_c=t_ctx, mod_row=bsz, tag="ctx")
    return xc
```
